```python
import math
import jax
import jax.numpy as jnp
from jax import lax
import numpy as np

D_MODEL = 1024
BATCH = 16
SEQ = 2048
DEPTH = 2

GRID_W = 64
CTX_LEN = 256
HEAD_DIM = 64
ROPE_THETA = 10000.0
EPS = 1e-6
NEG_INF = -1e30
BLOCK = 128
CONV_WIDTH = 3
A_WIDTH = D_MODEL // 2
B_HEADS = D_MODEL // 256
B_QK_DIM = HEAD_DIM
B_V_DIM = 2 * HEAD_DIM
B_QK_WIDTH = B_HEADS * 2 * B_QK_DIM
B_V_WIDTH = B_HEADS * B_V_DIM
C_HEADS = D_MODEL // 128
C_KV_HEADS = C_HEADS // 4
C_GROUP = C_HEADS // C_KV_HEADS
C_Q_WIDTH = C_HEADS * HEAD_DIM
C_KV_WIDTH = C_KV_HEADS * HEAD_DIM
WINDOW = 128
POOL_SIZES = (2, 4, 8, 16)
D_WIDTH = D_MODEL // 2
POOL_GROUP = D_WIDTH // len(POOL_SIZES)
D_FF = ((8 * D_MODEL // 3 + 127) // 128) * 128
MIX_WIDTH = A_WIDTH + B_V_WIDTH
AB_COLS = 3 * A_WIDTH + 2 * B_QK_WIDTH + B_V_WIDTH
CD_COLS = C_Q_WIDTH + 2 * C_KV_WIDTH + D_WIDTH
N_EVEN = (DEPTH + 1) // 2
N_ODD = DEPTH // 2

kernel_name = 'hybrid_flow_backbone_block'


def rms_norm(x, w=None):
    xf = x.astype(jnp.float32)
    y = xf * lax.rsqrt(jnp.mean(xf * xf, axis=-1, keepdims=True) + EPS)
    if w is not None:
        y = y * w.astype(jnp.float32)
    return y.astype(x.dtype)


def modulate(xn, shift, scale):
    return xn * (1.0 + scale) + shift


def ada_mods(cvec, w, b):
    return jnp.split(jax.nn.silu(cvec) @ w + b, 6, axis=-1)


def lambda_init(layer):
    return 0.8 - 0.6 * math.exp(-0.3 * layer)


def dwconv3(x, w):
    xp = jnp.pad(x, ((0, 0), (1, 1), (0, 0)))
    return xp[:, :-2] * w[0] + xp[:, 1:-1] * w[1] + xp[:, 2:] * w[2]


def axial_rope_angles(n):
    rows = n // GRID_W
    row = jnp.repeat(jnp.arange(rows, dtype=jnp.int32), GRID_W)
    col = jnp.tile(jnp.arange(GRID_W, dtype=jnp.int32), rows)
    axis_dim = HEAD_DIM // 2
    inv = 1.0 / (ROPE_THETA ** (jnp.arange(0, axis_dim, 2, dtype=jnp.float32) / axis_dim))
    return row.astype(jnp.float32)[:, None] * inv, col.astype(jnp.float32)[:, None] * inv


def rope_1d(x, ang):
    x1, x2 = jnp.split(x, 2, axis=-1)
    cos = jnp.cos(ang).astype(x.dtype)
    sin = jnp.sin(ang).astype(x.dtype)
    return jnp.concatenate([x1 * cos - x2 * sin, x1 * sin + x2 * cos], axis=-1)


def apply_axial_rope(x, ang_row, ang_col):
    shape = (x.shape[1],) + (1,) * (x.ndim - 3) + (ang_row.shape[-1],)
    xr, xc = jnp.split(x, 2, axis=-1)
    return jnp.concatenate([rope_1d(xr, ang_row.reshape(shape)), rope_1d(xc, ang_col.reshape(shape))], axis=-1)


def diff_attend(q, k, v, lam):
    s = jnp.einsum('bqhmd,bkhmd->bhmqk', q, k).astype(jnp.float32) * (B_QK_DIM ** -0.5)
    p = jax.nn.softmax(s, axis=-1)
    w = (p[:, :, 0] - lam * p[:, :, 1]).astype(v.dtype)
    return jnp.einsum('bhqk,bkhe->bqhe', w, v)


def sink_attend(q, ks, vs, masks, sink):
    b, nq = q.shape[0], q.shape[1]
    logits = [jnp.broadcast_to(sink[None, :, :, None, None], (b, C_KV_HEADS, C_GROUP, nq, 1))]
    for k, m in zip(ks, masks):
        s = jnp.einsum('bqhgd,bkhd->bhgqk', q, k).astype(jnp.float32)
        if m is not None:
            s = jnp.where(m, s, NEG_INF)
        logits.append(s)
    p = jax.nn.softmax(jnp.concatenate(logits, axis=-1), axis=-1)
    terms = []
    start = 1
    for k, v in zip(ks, vs):
        pk = lax.slice_in_dim(p, start, start + k.shape[1], axis=-1).astype(v.dtype)
        terms.append(jnp.einsum('bhgqk,bkhd->bqhgd', pk, v))
        start += k.shape[1]
    return sum(terms)


def multiscale_pool(p, w_pool, scale):
    b, n, _ = p.shape
    pf = p.astype(jnp.float32)
    cs = jnp.concatenate([jnp.zeros((b, 1, D_WIDTH), jnp.float32), jnp.cumsum(pf, axis=1)], axis=1)
    t = jnp.arange(n, dtype=jnp.int32)
    outs = []
    for g, w in enumerate(POOL_SIZES):
        lo = jnp.clip(t - w // 2, 0, n)
        hi = jnp.clip(t + w // 2, 0, n)
        sl = slice(g * POOL_GROUP, (g + 1) * POOL_GROUP)
        csg = cs[:, :, sl]
        mean = (csg[:, hi] - csg[:, lo]) / (hi - lo).astype(jnp.float32)[:, None]
        outs.append(mean - pf[:, :, sl])
    z = jnp.stack(outs, axis=2).astype(p.dtype)
    z = jnp.einsum('bngc,gce->bnge', z, w_pool).reshape(b, n, D_WIDTH)
    return z * scale


def conv_ffn(h, w_up, conv_w, conv_b, w_down):
    u = dwconv3(h @ w_up, conv_w) + conv_b
    a, g = jnp.split(u, 2, axis=-1)
    return (jax.nn.silu(g) * a) @ w_down


def mixer_ab(hl, hc, w_in, conv_w, lam_qk, subln_w, w_out, lam_init, ang_row, ang_col, ctx_out):
    b, n, _ = hl.shape
    lc = hc.shape[1]
    splits = [A_WIDTH, 2 * A_WIDTH, 3 * A_WIDTH, 3 * A_WIDTH + B_QK_WIDTH, 3 * A_WIDTH + 2 * B_QK_WIDTH]
    gb, gc, hv, q, k, v = jnp.split(hl @ w_in, splits, axis=-1)
    gbc, gcc, hvc, qc, kc, vc = jnp.split(hc @ w_in, splits, axis=-1)
    lq = lam_qk.astype(jnp.float32)
    lam = jnp.exp(jnp.sum(lq[0] * lq[1])) - jnp.exp(jnp.sum(lq[2] * lq[3])) + lam_init
    ql = apply_axial_rope(q.reshape(b, n, B_HEADS, 2, B_QK_DIM), ang_row, ang_col)
    kl = apply_axial_rope(k.reshape(b, n, B_HEADS, 2, B_QK_DIM), ang_row, ang_col)
    vl = v.reshape(b, n, B_HEADS, B_V_DIM)
    kcx = kc.reshape(b, lc, B_HEADS, 2, B_QK_DIM)
    vcx = vc.reshape(b, lc, B_HEADS, B_V_DIM)
    k_all = jnp.concatenate([kcx, kl], axis=1)
    v_all = jnp.concatenate([vcx, vl], axis=1)
    nb = n // BLOCK
    qb = ql.reshape(b, nb, BLOCK, B_HEADS, 2, B_QK_DIM).swapaxes(0, 1)
    ob = lax.map(lambda qi: diff_attend(qi, k_all, v_all, lam), qb)
    yb = ob.swapaxes(0, 1).reshape(b, n, B_HEADS, B_V_DIM)
    yb = (rms_norm(yb, subln_w) * (1.0 - lam_init)).reshape(b, n, B_V_WIDTH)
    ya = gb * dwconv3(gc * hv, conv_w)
    yl = jnp.concatenate([ya, yb], axis=-1) @ w_out
    yc = None
    if ctx_out:
        ybc = diff_attend(qc.reshape(b, lc, B_HEADS, 2, B_QK_DIM), kcx, vcx, lam)
        ybc = (rms_norm(ybc, subln_w) * (1.0 - lam_init)).reshape(b, lc, B_V_WIDTH)
        yac = gbc * dwconv3(gcc * hvc, conv_w)
        yc = jnp.concatenate([yac, ybc], axis=-1) @ w_out
    return yl, yc


def mixer_cd(hl, hc, w_in, sink, pool_w, pool_scale, w_out, ang_row, ang_col, ctx_out):
    b, n, _ = hl.shape
    lc = hc.shape[1]
    splits = [C_Q_WIDTH, C_Q_WIDTH + C_KV_WIDTH, C_Q_WIDTH + 2 * C_KV_WIDTH]
    q, k, v, pl = jnp.split(hl @ w_in, splits, axis=-1)
    qc, kc, vc, pc = jnp.split(hc @ w_in, splits, axis=-1)
    scale = HEAD_DIM ** -0.5
    sink = sink.astype(jnp.float32).reshape(C_KV_HEADS, C_GROUP)
    ql = apply_axial_rope(q.reshape(b, n, C_KV_HEADS, C_GROUP, HEAD_DIM), ang_row, ang_col) * scale
    kl = apply_axial_rope(k.reshape(b, n, C_KV_HEADS, HEAD_DIM), ang_row, ang_col)
    vl = v.reshape(b, n, C_KV_HEADS, HEAD_DIM)
    kcx = kc.reshape(b, lc, C_KV_HEADS, HEAD_DIM)
    vcx = vc.reshape(b, lc, C_KV_HEADS, HEAD_DIM)
    pad = ((0, 0), (WINDOW, WINDOW), (0, 0), (0, 0))
    k_pad = jnp.pad(kl, pad)
    v_pad = jnp.pad(vl, pad)
    band = BLOCK + 2 * WINDOW
    nb = n // BLOCK
    qb = ql.reshape(b, nb, BLOCK, C_KV_HEADS, C_GROUP, HEAD_DIM).swapaxes(0, 1)

    def block(args):
        i, qi = args
        kb = lax.dynamic_slice_in_dim(k_pad, i * BLOCK, band, axis=1)
        vb = lax.dynamic_slice_in_dim(v_pad, i * BLOCK, band, axis=1)
        qpos = i * BLOCK + jnp.arange(BLOCK, dtype=jnp.int32)
        kpos = i * BLOCK - WINDOW + jnp.arange(band, dtype=jnp.int32)
        mask = (jnp.abs(kpos[None, :] - qpos[:, None]) <= WINDOW) & ((kpos >= 0) & (kpos < n))[None, :]
        return sink_attend(qi, [kcx, kb], [vcx, vb], [None, mask], sink)

    ob = lax.map(block, (jnp.arange(nb, dtype=jnp.int32), qb))
    ya = ob.swapaxes(0, 1).reshape(b, n, C_Q_WIDTH)
    yd = multiscale_pool(pl, pool_w, pool_scale)
    yl = jnp.concatenate([ya, yd], axis=-1) @ w_out
    yc = None
    if ctx_out:
        qcx = qc.reshape(b, lc, C_KV_HEADS, C_GROUP, HEAD_DIM) * scale
        yac = sink_attend(qcx, [kcx], [vcx], [None], sink).reshape(b, lc, C_Q_WIDTH)
        ydc = multiscale_pool(pc, pool_w, pool_scale)
        yc = jnp.concatenate([yac, ydc], axis=-1) @ w_out
    return yl, yc


def setup_inputs(seed: int = 0) -> dict:
    key = jax.random.key(seed)
    ks = jax.random.split(key, 24)

    def nrm(k, shape, s):
        return jax.random.normal(k, shape, jnp.float32) * s

    D = D_MODEL
    return {
        'x': nrm(ks[0], (BATCH, SEQ, D), 1.0),
        'c': nrm(ks[1], (BATCH, D), 1.0),
        'ctx': nrm(ks[2], (BATCH, CTX_LEN, D), 1.0),
        'c_ctx': nrm(ks[3], (D,), 1.0),
        'w_mod': nrm(ks[4], (DEPTH, D, 6 * D), 0.5 * D ** -0.5),
        'b_mod': nrm(ks[5], (DEPTH, 6 * D), 0.01),
        'w_in_ab': nrm(ks[6], (N_EVEN, D, AB_COLS), D ** -0.5),
        'conv_a': nrm(ks[7], (N_EVEN, CONV_WIDTH, A_WIDTH), CONV_WIDTH ** -0.5),
        'lam_qk': nrm(ks[8], (N_EVEN, 4, B_QK_DIM), 0.1),
        'subln_b': 1.0 + nrm(ks[9], (N_EVEN, B_V_DIM), 0.02),
        'w_out_ab': nrm(ks[10], (N_EVEN, MIX_WIDTH, D), MIX_WIDTH ** -0.5),
        'w_in_cd': nrm(ks[11], (N_ODD, D, CD_COLS), D ** -0.5),
        'sink_c': nrm(ks[12], (N_ODD, C_HEADS), 0.5),
        'pool_w': nrm(ks[13], (N_ODD, len(POOL_SIZES), POOL_GROUP, POOL_GROUP), POOL_GROUP ** -0.5),
        'pool_scale': 1.0 + nrm(ks[14], (N_ODD, D_WIDTH), 0.1),
        'w_out_cd': nrm(ks[15], (N_ODD, MIX_WIDTH, D), MIX_WIDTH ** -0.5),
        'w_up': nrm(ks[16], (DEPTH, D, 2 * D_FF), D ** -0.5),
        'conv_ffn_w': nrm(ks[17], (DEPTH, CONV_WIDTH, 2 * D_FF), CONV_WIDTH ** -0.5),
        'conv_ffn_b': nrm(ks[18], (DEPTH, 2 * D_FF), 0.01),
        'w_down': nrm(ks[19], (DEPTH, D_FF, D), D_FF ** -0.5),
        'final_norm_w': 1.0 + nrm(ks[20], (D,), 0.02),
    }


def reference(x, c, ctx, c_ctx, w_mod, b_mod, w_in_ab, conv_a, lam_qk, subln_b, w_out_ab, w_in_cd, sink_c, pool_w, pool_scale, w_out_cd, w_up, conv_ffn_w, conv_ffn_b, w_down, final_norm_w):
    ang_row, ang_col = axial_rope_angles(x.shape[1])
    xl, xc = x, ctx
    for l in range(DEPTH):
        ctx_out = l < DEPTH - 1
        ml = [m[:, None, :] for m in ada_mods(c, w_mod[l], b_mod[l])]
        mc = ada_mods(c_ctx, w_mod[l], b_mod[l])
        hl = modulate(rms_norm(xl), ml[0], ml[1])
        hc = modulate(rms_norm(xc), mc[0], mc[1])
        if l % 2 == 0:
            e = l // 2
            yl, yc = mixer_ab(hl, hc, w_in_ab[e], conv_a[e], lam_qk[e], subln_b[e], w_out_ab[e], lambda_init(l), ang_row, ang_col, ctx_out)
        else:
            o = l // 2
            yl, yc = mixer_cd(hl, hc, w_in_cd[o], sink_c[o], pool_w[o], pool_scale[o], w_out_cd[o], ang_row, ang_col, ctx_out)
        xl = xl + ml[2] * yl
        ffn = (w_up[l], conv_ffn_w[l], conv_ffn_b[l], w_down[l])
        xl = xl + ml[5] * conv_ffn(modulate(rms_norm(xl), ml[3], ml[4]), *ffn)
        if ctx_out:
            xc = xc + mc[2] * yc
            xc = xc + mc[5] * conv_ffn(modulate(rms_norm(xc), mc[3], mc[4]), *ffn)
    return rms_norm(xl, final_norm_w)
```

```python
import functools
import math

import jax
import jax.numpy as jnp
from jax import lax
from jax.experimental import pallas as pl
from jax.experimental.pallas import tpu as pltpu

F32 = jnp.float32
BF16 = jnp.bfloat16

D_MODEL = 1024
DEPTH = 2
GRID_W = 64
HEAD_DIM = 64
ROPE_THETA = 10000.0
EPS = 1e-6
NEG_INF = -1e30
WINDOW = 128
A_WIDTH = D_MODEL // 2
B_HEADS = D_MODEL // 256
B_V_DIM = 2 * HEAD_DIM
C_HEADS = D_MODEL // 128
C_KV_HEADS = C_HEADS // 4
C_Q_WIDTH = C_HEADS * HEAD_DIM
C_KV_WIDTH = C_KV_HEADS * HEAD_DIM
POOL_SIZES = (2, 4, 8, 16)
D_WIDTH = D_MODEL // 2
POOL_GROUP = D_WIDTH // len(POOL_SIZES)
D_FF = ((8 * D_MODEL // 3 + 127) // 128) * 128

LANES = 128
HALO = 8
LOG2E = 1.4426950408889634
QSCALE = (HEAD_DIM ** -0.5) * LOG2E
VMEM_LIMIT = 56 * 1024 * 1024

TM_PROJ = 512
TQ_ATT = 256
TM_FFN = 512
FF_CHUNK = 256


def _lambda_init(layer):
    return 0.8 - 0.6 * math.exp(-0.3 * layer)


def _cparams(n_axes):
    return pltpu.CompilerParams(dimension_semantics=("arbitrary",) * n_axes,
                                vmem_limit_bytes=VMEM_LIMIT)


def _const_spec(shape):
    nd = len(shape)
    return pl.BlockSpec(shape, lambda *_: (0,) * nd, pipeline_mode=pl.Buffered(1))


def _norm_mod(x, shift, scale):
    ms = jnp.mean(x * x, axis=-1, keepdims=True)
    return x * lax.rsqrt(ms + EPS) * (1.0 + scale) + shift


def _sigmoid(x):
    return 1.0 / (1.0 + jnp.exp(-x))


def _rope(x, c, s1, s2):
    outs = []
    for j in range(x.shape[1] // LANES):
        xc = x[:, j * LANES:(j + 1) * LANES]
        outs.append(xc * c + pltpu.roll(xc, LANES - 16, 1) * s1 + pltpu.roll(xc, 16, 1) * s2)
    return outs[0] if len(outs) == 1 else jnp.concatenate(outs, axis=1)


def _rope_tables(n):
    t = jnp.arange(n, dtype=jnp.int32)
    row = (t // GRID_W).astype(F32)
    col = (t % GRID_W).astype(F32)
    axis_dim = HEAD_DIM // 2
    inv = 1.0 / (ROPE_THETA ** (jnp.arange(0, axis_dim, 2, dtype=F32) / axis_dim))
    ar = row[:, None] * inv
    ac = col[:, None] * inv
    cr, sr, cc, sc = jnp.cos(ar), jnp.sin(ar), jnp.cos(ac), jnp.sin(ac)
    z = jnp.zeros_like(cr)
    c64 = jnp.concatenate([cr, cr, cc, cc], axis=-1)
    s1 = jnp.concatenate([-sr, z, -sc, z], axis=-1)
    s2 = jnp.concatenate([z, sr, z, sc], axis=-1)
    reps = LANES // HEAD_DIM
    return tuple(jnp.tile(a, (1, reps)) for a in (c64, s1, s2))


def _mods_kernel(c_ref, w_ref, b_ref, o_ref):
    cv = c_ref[...]
    act = (cv * _sigmoid(cv)).astype(BF16)
    o_ref[0] = jnp.dot(act, w_ref[0], preferred_element_type=F32) + b_ref[0]


def _mods(cvec, w_mod, b_mod):
    rows = cvec.shape[0]
    nblk = 6
    return pl.pallas_call(
        _mods_kernel,
        grid=(DEPTH, nblk),
        in_specs=[
            pl.BlockSpec((rows, D_MODEL), lambda l, j: (0, 0)),
            pl.BlockSpec((1, D_MODEL, D_MODEL), lambda l, j: (l, 0, j)),
            pl.BlockSpec((1, 1, D_MODEL), lambda l, j: (l, 0, j)),
        ],
        out_specs=pl.BlockSpec((1, rows, D_MODEL), lambda l, j: (l, 0, j)),
        out_shape=jax.ShapeDtypeStruct((DEPTH, rows, 6 * D_MODEL), F32),
        compiler_params=_cparams(2),
        name="mods",
    )(cvec, w_mod, b_mod.reshape(DEPTH, 1, 6 * D_MODEL))


def _proj_ab_kernel(*refs, rope):
    if rope:
        x_ref, sh_ref, sc_ref, w_ref, c_ref, s1_ref, s2_ref, gb_ref, p_ref, q_ref, kt_ref, v_ref = refs
    else:
        x_ref, sh_ref, sc_ref, w_ref, gb_ref, p_ref, q_ref, kt_ref, v_ref = refs
    h = _norm_mod(x_ref[0], sh_ref[0], sc_ref[0]).astype(BF16)

    def proj(j):
        return jnp.dot(h, w_ref[:, j * A_WIDTH:(j + 1) * A_WIDTH], preferred_element_type=F32)

    gb_ref[0] = proj(0)
    p_ref[0] = proj(1) * proj(2)
    q = proj(3)
    k = proj(4)
    if rope:
        c, s1, s2 = c_ref[...], s1_ref[...], s2_ref[...]
        q = _rope(q, c, s1, s2)
        k = _rope(k, c, s1, s2)
    q = q * QSCALE
    first_map = (lax.broadcasted_iota(jnp.int32, q.shape, 1) & HEAD_DIM) == 0
    q_ref[0, 0] = jnp.where(first_map, q, 0.0).astype(BF16)
    q_ref[0, 1] = jnp.where(first_map, 0.0, q).astype(BF16)
    kt_ref[0] = k.T.astype(BF16)
    v = proj(5).astype(BF16)
    ones = jnp.ones((v.shape[0], LANES), BF16)
    pieces = []
    for hd in range(B_HEADS):
        pieces += [v[:, hd * B_V_DIM:(hd + 1) * B_V_DIM], ones]
    v_ref[0] = jnp.concatenate(pieces, axis=1)


def _proj_ab(x, shift, scale, w, tables, tm):
    b, s, _ = x.shape
    nt = s // tm
    rope = tables is not None
    per_batch = shift.shape[0] > 1
    mod_spec = pl.BlockSpec((1, 1, D_MODEL), (lambda i, bb: (bb, 0, 0)) if per_batch else (lambda i, bb: (0, 0, 0)))
    in_specs = [pl.BlockSpec((1, tm, D_MODEL), lambda i, bb: (bb, i, 0)), mod_spec, mod_spec,
                _const_spec(w.shape)]
    args = [x, shift, scale, w]
    if rope:
        in_specs += [pl.BlockSpec((tm, LANES), lambda i, bb: (i, 0))] * 3
        args += list(tables)
    out_shape = (
        jax.ShapeDtypeStruct((b, s, A_WIDTH), F32),
        jax.ShapeDtypeStruct((b, s, A_WIDTH), F32),
        jax.ShapeDtypeStruct((b, 2, s, A_WIDTH), BF16),
        jax.ShapeDtypeStruct((b, A_WIDTH, s), BF16),
        jax.ShapeDtypeStruct((b, s, 2 * A_WIDTH), BF16),
    )
    out_specs = (
        pl.BlockSpec((1, tm, A_WIDTH), lambda i, bb: (bb, i, 0)),
        pl.BlockSpec((1, tm, A_WIDTH), lambda i, bb: (bb, i, 0)),
        pl.BlockSpec((1, 2, tm, A_WIDTH), lambda i, bb: (bb, 0, i, 0)),
        pl.BlockSpec((1, A_WIDTH, tm), lambda i, bb: (bb, 0, i)),
        pl.BlockSpec((1, tm, 2 * A_WIDTH), lambda i, bb: (bb, i, 0)),
    )
    return pl.pallas_call(
        functools.partial(_proj_ab_kernel, rope=rope),
        grid=(nt, b), in_specs=in_specs, out_specs=out_specs, out_shape=out_shape,
        compiler_params=_cparams(2), name="proj_ab_rope" if rope else "proj_ab",
    )(*args)


def _halo_conv3(pm, prev_row, next_row, w):
    rows = pm.shape[0]
    ridx = lax.broadcasted_iota(jnp.int32, pm.shape, 0)
    dn = jnp.where(ridx == 0, prev_row, pltpu.roll(pm, 1, 0))
    up = jnp.where(ridx == rows - 1, next_row, pltpu.roll(pm, rows - 1, 0))
    return dn * w[0:1] + pm * w[1:2] + up * w[2:3]


def _mix_ab_kernel(*refs, has_lat, lam_init):
    if has_lat:
        (q_ref, ktl_ref, vl_ref, ktc_ref, vc_ref, gb_ref, p_ref, pp_ref, pn_ref, cw_ref, lq_ref, sl_ref,
         wo_ref, x_ref, g_ref, o_ref) = refs
        keys = ((ktl_ref, vl_ref), (ktc_ref, vc_ref))
    else:
        (q_ref, ktc_ref, vc_ref, gb_ref, p_ref, pp_ref, pn_ref, cw_ref, lq_ref, sl_ref,
         wo_ref, x_ref, g_ref, o_ref) = refs
        keys = ((ktc_ref, vc_ref),)
    i = pl.program_id(1)
    last = pl.num_programs(1) - 1
    tq = x_ref.shape[1]

    lq = lq_ref[...]
    lam = (jnp.exp(jnp.sum(lq[0:1] * lq[1:2], axis=-1, keepdims=True))
           - jnp.exp(jnp.sum(lq[2:3] * lq[3:4], axis=-1, keepdims=True)) + lam_init)

    yb = []
    for hd in range(B_HEADS):
        lo, hi = hd * B_V_DIM, (hd + 1) * B_V_DIM
        qs = q_ref[0, :, :, lo:hi].reshape(2 * tq, B_V_DIM)
        s = [jnp.dot(qs, kt[0, lo:hi, :], preferred_element_type=F32) for kt, _ in keys]
        m = jnp.max(s[0], axis=-1, keepdims=True)
        for sp in s[1:]:
            m = jnp.maximum(m, jnp.max(sp, axis=-1, keepdims=True))
        o = None
        for sp, (_, vr) in zip(s, keys):
            pr = jnp.exp2(sp - m).astype(BF16)
            t = jnp.dot(pr, vr[0, :, 2 * lo:2 * hi], preferred_element_type=F32)
            o = t if o is None else o + t
        o1 = o[:tq, :B_V_DIM] / o[:tq, B_V_DIM:B_V_DIM + 1]
        o2 = o[tq:, :B_V_DIM] / o[tq:, B_V_DIM:B_V_DIM + 1]
        oh = o1 - lam * o2
        ms = jnp.mean(oh * oh, axis=-1, keepdims=True)
        yb.append(oh * lax.rsqrt(ms + EPS) * sl_ref[...] * (1.0 - lam_init))

    prev_row = jnp.where(i > 0, pp_ref[0, HALO - 1:HALO, :], 0.0)
    next_row = jnp.where(i < last, pn_ref[0, 0:1, :], 0.0)
    ya = gb_ref[0] * _halo_conv3(p_ref[0], prev_row, next_row, cw_ref[...])

    y = jnp.concatenate([ya] + yb, axis=1).astype(BF16)
    y = jnp.dot(y, wo_ref[...], preferred_element_type=F32)
    o_ref[0] = x_ref[0] + g_ref[0] * y


def _mix_ab(q2, lat_kv, ctx_kv, gb, p, conv_w, lam_qk, subln_w, w_out, x, gate, lam_init, tq):
    b, s, _ = x.shape
    nq = s // tq
    nh = s // HALO
    per_batch = gate.shape[0] > 1
    has_lat = lat_kv is not None
    bmap = lambda bb, i: (bb, 0, 0)
    in_specs = [pl.BlockSpec((1, 2, tq, A_WIDTH), lambda bb, i: (bb, 0, i, 0))]
    args = [q2]
    for kv in ([lat_kv] if has_lat else []) + [ctx_kv]:
        kt, vx = kv
        in_specs += [pl.BlockSpec((1,) + kt.shape[1:], bmap), pl.BlockSpec((1,) + vx.shape[1:], bmap)]
        args += [kt, vx]
    tile = pl.BlockSpec((1, tq, A_WIDTH), lambda bb, i: (bb, i, 0))
    in_specs += [
        tile, tile,
        pl.BlockSpec((1, HALO, A_WIDTH), lambda bb, i: (bb, jnp.maximum(i * (tq // HALO) - 1, 0), 0)),
        pl.BlockSpec((1, HALO, A_WIDTH), lambda bb, i: (bb, jnp.minimum((i + 1) * (tq // HALO), nh - 1), 0)),
        _const_spec(conv_w.shape), _const_spec(lam_qk.shape), _const_spec((1, B_V_DIM)),
        _const_spec(w_out.shape),
        pl.BlockSpec((1, tq, D_MODEL), lambda bb, i: (bb, i, 0)),
        pl.BlockSpec((1, 1, D_MODEL), bmap if per_batch else (lambda bb, i: (0, 0, 0))),
    ]
    args += [gb, p, p, p, conv_w, lam_qk, subln_w.reshape(1, B_V_DIM), w_out, x, gate]
    return pl.pallas_call(
        functools.partial(_mix_ab_kernel, has_lat=has_lat, lam_init=lam_init),
        grid=(b, nq), in_specs=in_specs,
        out_specs=pl.BlockSpec((1, tq, D_MODEL), lambda bb, i: (bb, i, 0)),
        out_shape=jax.ShapeDtypeStruct(x.shape, F32),
        compiler_params=_cparams(2), name="mix_ab_lat" if has_lat else "mix_ab_ctx",
    )(*args)


def _ffn_kernel(*refs, final_norm):
    if final_norm:
        (x_ref, xp_ref, xn_ref, sh_ref, sc_ref, g_ref, wu_ref, cw_ref, cb_ref, wd_ref, fw_ref,
         o_ref, u_scr, act_scr) = refs
    else:
        (x_ref, xp_ref, xn_ref, sh_ref, sc_ref, g_ref, wu_ref, cw_ref, cb_ref, wd_ref,
         o_ref, u_scr, act_scr) = refs
    i = pl.program_id(1)
    last = pl.num_programs(1) - 1
    tm = x_ref.shape[1]
    sh, sc = sh_ref[0], sc_ref[0]
    x = x_ref[0]
    hp = jnp.where(i > 0, _norm_mod(xp_ref[0], sh, sc), 0.0)
    hn = jnp.where(i < last, _norm_mod(xn_ref[0], sh, sc), 0.0)
    hext = jnp.concatenate([hp, _norm_mod(x, sh, sc), hn], axis=0).astype(BF16)

    for c in range(D_FF // FF_CHUNK):
        halves = []
        for part in range(2):
            col = part * D_FF + c * FF_CHUNK
            slot = 2 * (c % 2) + part
            u_scr[slot] = jnp.dot(hext, wu_ref[:, col:col + FF_CHUNK], preferred_element_type=F32)
            cw = cw_ref[:, col:col + FF_CHUNK]
            halves.append(u_scr[slot, HALO - 1:HALO - 1 + tm, :] * cw[0:1]
                          + u_scr[slot, HALO:HALO + tm, :] * cw[1:2]
                          + u_scr[slot, HALO + 1:HALO + 1 + tm, :] * cw[2:3]
                          + cb_ref[:, col:col + FF_CHUNK])
        a, g = halves
        act_scr[:, c * FF_CHUNK:(c + 1) * FF_CHUNK] = (g * _sigmoid(g) * a).astype(BF16)

    y = jnp.dot(act_scr[...], wd_ref[...], preferred_element_type=F32)
    out = x + g_ref[0] * y
    if final_norm:
        ms = jnp.mean(out * out, axis=-1, keepdims=True)
        out = out * lax.rsqrt(ms + EPS) * fw_ref[...]
    o_ref[0] = out


def _ffn(x, shift, scale, gate, w_up, conv_w, conv_b, w_down, final_w, tm):
    b, s, _ = x.shape
    nt = s // tm
    nh = s // HALO
    per_batch = shift.shape[0] > 1
    final_norm = final_w is not None
    mod_spec = pl.BlockSpec((1, 1, D_MODEL), (lambda bb, i: (bb, 0, 0)) if per_batch else (lambda bb, i: (0, 0, 0)))
    in_specs = [
        pl.BlockSpec((1, tm, D_MODEL), lambda bb, i: (bb, i, 0)),
        pl.BlockSpec((1, HALO, D_MODEL), lambda bb, i: (bb, jnp.maximum(i * (tm // HALO) - 1, 0), 0)),
        pl.BlockSpec((1, HALO, D_MODEL), lambda bb, i: (bb, jnp.minimum((i + 1) * (tm // HALO), nh - 1), 0)),
        mod_spec, mod_spec, mod_spec,
        _const_spec(w_up.shape), _const_spec(conv_w.shape), _const_spec((1, 2 * D_FF)), _const_spec(w_down.shape),
    ]
    args = [x, x, x, shift, scale, gate, w_up, conv_w, conv_b.reshape(1, 2 * D_FF), w_down]
    if final_norm:
        in_specs.append(_const_spec((1, D_MODEL)))
        args.append(final_w.reshape(1, D_MODEL))
    return pl.pallas_call(
        functools.partial(_ffn_kernel, final_norm=final_norm),
        grid=(b, nt), in_specs=in_specs,
        out_specs=pl.BlockSpec((1, tm, D_MODEL), lambda bb, i: (bb, i, 0)),
        out_shape=jax.ShapeDtypeStruct(x.shape, F32),
        scratch_shapes=[pltpu.VMEM((4, tm + 2 * HALO, FF_CHUNK), F32), pltpu.VMEM((tm, D_FF), BF16)],
        compiler_params=_cparams(2), name="ffn_final" if final_norm else "ffn",
    )(*args)


def _dup_halves(a):
    sw = pltpu.roll(a, HEAD_DIM, 1)
    first = lax.broadcasted_iota(jnp.int32, a.shape, 1) < HEAD_DIM
    return jnp.where(first, a, sw), jnp.where(first, sw, a)


def _proj_cd_kernel(*refs, rope, kv_only):
    if kv_only:
        x_ref, sh_ref, sc_ref, w_ref, kt_ref, v_ref = refs
    else:
        x_ref, sh_ref, sc_ref, w_ref, c_ref, s1_ref, s2_ref, q_ref, kt_ref, v_ref, pl_ref = refs
    h = _norm_mod(x_ref[0], sh_ref[0], sc_ref[0]).astype(BF16)
    kv = jnp.dot(h, w_ref[:, C_Q_WIDTH:C_Q_WIDTH + 2 * C_KV_WIDTH], preferred_element_type=F32)
    k = kv[:, :C_KV_WIDTH]
    v = kv[:, C_KV_WIDTH:]
    if rope:
        c, s1, s2 = c_ref[...], s1_ref[...], s2_ref[...]
        k = _rope(k, c, s1, s2)
    k0, k1 = _dup_halves(k)
    kt_ref[0, 0] = k0.T.astype(BF16)
    kt_ref[0, 1] = k1.T.astype(BF16)
    v0, v1 = _dup_halves(v)
    ones = jnp.ones(v.shape, BF16)
    v_ref[0] = jnp.concatenate([v0.astype(BF16), ones, v1.astype(BF16), ones], axis=1)
    if not kv_only:
        q = jnp.dot(h, w_ref[:, :C_Q_WIDTH], preferred_element_type=F32)
        if rope:
            q = _rope(q, c, s1, s2)
        q_ref[0] = (q * QSCALE).astype(BF16)
        pl_ref[0] = jnp.dot(h, w_ref[:, C_Q_WIDTH + 2 * C_KV_WIDTH:], preferred_element_type=F32)


def _proj_cd(x, shift, scale, w, tables, tm):
    b, s, _ = x.shape
    nt = s // tm
    kv_only = tables is None
    per_batch = shift.shape[0] > 1
    mod_spec = pl.BlockSpec((1, 1, D_MODEL), (lambda i, bb: (bb, 0, 0)) if per_batch else (lambda i, bb: (0, 0, 0)))
    in_specs = [pl.BlockSpec((1, tm, D_MODEL), lambda i, bb: (bb, i, 0)), mod_spec, mod_spec,
                _const_spec(w.shape)]
    args = [x, shift, scale, w]
    kv_shapes = (
        jax.ShapeDtypeStruct((b, C_KV_HEADS, LANES, s), BF16),
        jax.ShapeDtypeStruct((b, s, 2 * C_KV_HEADS * LANES), BF16),
    )
    kv_specs = (
        pl.BlockSpec((1, C_KV_HEADS, LANES, tm), lambda i, bb: (bb, 0, 0, i)),
        pl.BlockSpec((1, tm, 2 * C_KV_HEADS * LANES), lambda i, bb: (bb, i, 0)),
    )
    if kv_only:
        out_shape, out_specs = kv_shapes, kv_specs
    else:
        in_specs += [pl.BlockSpec((tm, LANES), lambda i, bb: (i, 0))] * 3
        args += list(tables)
        out_shape = (jax.ShapeDtypeStruct((b, s, C_Q_WIDTH), BF16),) + kv_shapes + (
            jax.ShapeDtypeStruct((b, s, D_WIDTH), F32),)
        out_specs = (pl.BlockSpec((1, tm, C_Q_WIDTH), lambda i, bb: (bb, i, 0)),) + kv_specs + (
            pl.BlockSpec((1, tm, D_WIDTH), lambda i, bb: (bb, i, 0)),)
    return pl.pallas_call(
        functools.partial(_proj_cd_kernel, rope=not kv_only, kv_only=kv_only),
        grid=(nt, b), in_specs=in_specs, out_specs=out_specs, out_shape=out_shape,
        compiler_params=_cparams(2), name="proj_cd_kv" if kv_only else "proj_cd",
    )(*args)


def _mix_cd_kernel(q_ref, ktp_ref, ktm_ref, ktn_ref, ktc_ref, vp_ref, vm_ref, vn_ref, vc_ref,
                   pl_ref, plp_ref, pln_ref, sink_ref, pw_ref, ps_ref, wo_ref, x_ref, g_ref, o_ref, pool_scr):
    i = pl.program_id(1)
    last = pl.num_programs(1) - 1
    tq = x_ref.shape[1]
    group = C_HEADS // C_KV_HEADS

    t = lax.broadcasted_iota(jnp.int32, (tq, tq + 2 * WINDOW), 0)
    cidx = lax.broadcasted_iota(jnp.int32, (tq, tq + 2 * WINDOW), 1)
    rel = cidx - WINDOW - t
    c_lo = jnp.where(i > 0, 0, WINDOW)
    c_hi = jnp.where(i < last, tq + 2 * WINDOW, tq + WINDOW)
    bias = jnp.where(jnp.abs(rel) <= WINDOW, 0.0, NEG_INF)
    bias = jnp.where(cidx >= c_lo, bias, NEG_INF)
    bias = jnp.where(cidx < c_hi, bias, NEG_INF)
    bias = jnp.concatenate([bias] * group, axis=0)

    q = q_ref[0]
    first = (lax.broadcasted_iota(jnp.int32, (tq, LANES), 1) < HEAD_DIM)
    zero = jnp.zeros((tq, LANES), BF16)
    ya = []
    for j in range(C_KV_HEADS):
        rows = []
        for cc in range(group // 2):
            qc = q[:, (j * (group // 2) + cc) * LANES:(j * (group // 2) + cc + 1) * LANES]
            rows += [jnp.where(first, qc, zero), jnp.where(first, zero, qc)]
        qs = jnp.concatenate(rows, axis=0)
        s_c = jnp.dot(qs, ktc_ref[0, j], preferred_element_type=F32)
        s_w = jnp.concatenate(
            [jnp.dot(qs, kr[0, j], preferred_element_type=F32) for kr in (ktp_ref, ktm_ref, ktn_ref)],
            axis=1) + bias
        sink = jnp.concatenate(
            [jnp.broadcast_to(sink_ref[:, j * group + r:j * group + r + 1] * LOG2E, (tq, 1)) for r in range(group)],
            axis=0)
        m = jnp.maximum(jnp.maximum(jnp.max(s_c, axis=-1, keepdims=True), jnp.max(s_w, axis=-1, keepdims=True)),
                        sink)
        p_c = jnp.exp2(s_c - m).astype(BF16)
        p_w = jnp.exp2(s_w - m).astype(BF16)
        vlo, vhi = j * 2 * LANES, (j + 1) * 2 * LANES
        o = jnp.dot(p_c, vc_ref[0, :, vlo:vhi], preferred_element_type=F32)
        o += jnp.dot(p_w[:, :WINDOW], vp_ref[0, :, vlo:vhi], preferred_element_type=F32)
        o += jnp.dot(p_w[:, WINDOW:WINDOW + tq], vm_ref[0, :, vlo:vhi], preferred_element_type=F32)
        o += jnp.dot(p_w[:, WINDOW + tq:], vn_ref[0, :, vlo:vhi], preferred_element_type=F32)
        den = o[:, LANES:LANES + 1] + jnp.exp2(sink - m)
        oh = o[:, :LANES] / den
        for cc in range(group // 2):
            ya.append(jnp.where(first, oh[(2 * cc) * tq:(2 * cc + 1) * tq], oh[(2 * cc + 1) * tq:(2 * cc + 2) * tq]))

    pm = pl_ref[0]
    pool_scr[0:HALO, :] = jnp.where(i > 0, plp_ref[0], 0.0)
    pool_scr[HALO:HALO + tq, :] = pm
    pool_scr[HALO + tq:, :] = jnp.where(i < last, pln_ref[0], 0.0)
    n = tq * pl.num_programs(1)
    pos = i * tq + lax.broadcasted_iota(jnp.int32, (tq, POOL_GROUP), 0)
    yd = []
    for gi, w in enumerate(POOL_SIZES):
        glo, ghi = gi * POOL_GROUP, (gi + 1) * POOL_GROUP
        acc = None
        for d in range(-(w // 2), w // 2):
            term = pool_scr[HALO + d:HALO + d + tq, glo:ghi]
            acc = term if acc is None else acc + term
        cnt = (jnp.minimum(pos + w // 2, n) - jnp.maximum(pos - w // 2, 0)).astype(F32)
        z = (acc / cnt - pm[:, glo:ghi]).astype(BF16)
        yd.append(jnp.dot(z, pw_ref[gi], preferred_element_type=F32) * ps_ref[:, glo:ghi])

    y = jnp.concatenate(ya + yd, axis=1).astype(BF16)
    y = jnp.dot(y, wo_ref[...], preferred_element_type=F32)
    o_ref[0] = x_ref[0] + g_ref[0] * y


def _mix_cd(q, kt, vx, ktc, vxc, plv, sink, pool_w, pool_scale, w_out, x, gate, tq):
    b, s, _ = x.shape
    nq = s // tq
    nh = s // HALO
    nw = s // WINDOW
    r = tq // WINDOW
    lc = ktc.shape[-1]
    vw = vx.shape[-1]
    prev_w = lambda bb, i: jnp.maximum(i * r - 1, 0)
    next_w = lambda bb, i: jnp.minimum((i + 1) * r, nw - 1)
    in_specs = [
        pl.BlockSpec((1, tq, C_Q_WIDTH), lambda bb, i: (bb, i, 0)),
        pl.BlockSpec((1, C_KV_HEADS, LANES, WINDOW), lambda bb, i: (bb, 0, 0, prev_w(bb, i))),
        pl.BlockSpec((1, C_KV_HEADS, LANES, tq), lambda bb, i: (bb, 0, 0, i)),
        pl.BlockSpec((1, C_KV_HEADS, LANES, WINDOW), lambda bb, i: (bb, 0, 0, next_w(bb, i))),
        pl.BlockSpec((1, C_KV_HEADS, LANES, lc), lambda bb, i: (bb, 0, 0, 0)),
        pl.BlockSpec((1, WINDOW, vw), lambda bb, i: (bb, prev_w(bb, i), 0)),
        pl.BlockSpec((1, tq, vw), lambda bb, i: (bb, i, 0)),
        pl.BlockSpec((1, WINDOW, vw), lambda bb, i: (bb, next_w(bb, i), 0)),
        pl.BlockSpec((1, lc, vw), lambda bb, i: (bb, 0, 0)),
        pl.BlockSpec((1, tq, D_WIDTH), lambda bb, i: (bb, i, 0)),
        pl.BlockSpec((1, HALO, D_WIDTH), lambda bb, i: (bb, jnp.maximum(i * (tq // HALO) - 1, 0), 0)),
        pl.BlockSpec((1, HALO, D_WIDTH), lambda bb, i: (bb, jnp.minimum((i + 1) * (tq // HALO), nh - 1), 0)),
        _const_spec((1, C_HEADS)), _const_spec(pool_w.shape), _const_spec((1, D_WIDTH)), _const_spec(w_out.shape),
        pl.BlockSpec((1, tq, D_MODEL), lambda bb, i: (bb, i, 0)),
        pl.BlockSpec((1, 1, D_MODEL), lambda bb, i: (bb, 0, 0)),
    ]
    args = [q, kt, kt, kt, ktc, vx, vx, vx, vxc, plv, plv, plv, sink.reshape(1, C_HEADS), pool_w,
            pool_scale.reshape(1, D_WIDTH), w_out, x, gate]
    return pl.pallas_call(
        _mix_cd_kernel,
        grid=(b, nq), in_specs=in_specs,
        out_specs=pl.BlockSpec((1, tq, D_MODEL), lambda bb, i: (bb, i, 0)),
        out_shape=jax.ShapeDtypeStruct(x.shape, F32),
        scratch_shapes=[pltpu.VMEM((tq + 2 * HALO, D_WIDTH), F32)],
        compiler_params=_cparams(2), name="mix_cd",
    )(*args)


def kernel(x, c, ctx, c_ctx, w_mod, b_mod, w_in_ab, conv_a, lam_qk, subln_b, w_out_ab, w_in_cd, sink_c, pool_w,
           pool_scale, w_out_cd, w_up, conv_ffn_w, conv_ffn_b, w_down, final_norm_w):
    b, s, d = x.shape
    lc = ctx.shape[1]
    assert d == D_MODEL and DEPTH == 2 and s % TM_PROJ == 0 and s % TM_FFN == 0 and s % TQ_ATT == 0

    rows = ((b + 1 + HALO - 1) // HALO) * HALO
    cvec = jnp.concatenate([c, c_ctx[None, :], jnp.zeros((rows - b - 1, d), F32)], axis=0)
    mods = _mods(cvec, w_mod.astype(BF16), b_mod)

    def split(l):
        ml = [mods[l, :b, j * d:(j + 1) * d].reshape(b, 1, d) for j in range(6)]
        mc = [mods[l, b:b + 1, j * d:(j + 1) * d].reshape(1, 1, d) for j in range(6)]
        return ml, mc

    tables = _rope_tables(s)
    bf = lambda a: a.astype(BF16)

    ml, mc = split(0)
    w_in = bf(w_in_ab[0])
    w_out = bf(w_out_ab[0])
    gb, p, q2, kt, vx = _proj_ab(x, ml[0], ml[1], w_in, tables, TM_PROJ)
    gbc, pc, q2c, ktc, vxc = _proj_ab(ctx, mc[0], mc[1], w_in, None, lc)
    lam0 = _lambda_init(0)
    xl = _mix_ab(q2, (kt, vx), (ktc, vxc), gb, p, conv_a[0], lam_qk[0], subln_b[0], w_out, x, ml[2], lam0, TQ_ATT)
    xc = _mix_ab(q2c, None, (ktc, vxc), gbc, pc, conv_a[0], lam_qk[0], subln_b[0], w_out, ctx, mc[2], lam0, lc)
    ffn_w = (bf(w_up[0]), conv_ffn_w[0], conv_ffn_b[0], bf(w_down[0]))
    xl = _ffn(xl, ml[3], ml[4], ml[5], *ffn_w, None, TM_FFN)
    xc = _ffn(xc, mc[3], mc[4], mc[5], *ffn_w, None, lc)

    ml, mc = split(1)
    w_in = bf(w_in_cd[0])
    q, kt, vx, plv = _proj_cd(xl, ml[0], ml[1], w_in, tables, TM_PROJ)
    ktc, vxc = _proj_cd(xc, mc[0], mc[1], w_in, None, lc)
    xl = _mix_cd(q, kt, vx, ktc, vxc, plv, sink_c[0], bf(pool_w[0]), pool_scale[0], bf(w_out_cd[0]), xl, ml[2],
                 TQ_ATT)
    return _ffn(xl, ml[3], ml[4], ml[5], bf(w_up[1]), conv_ffn_w[1], conv_ffn_b[1], bf(w_down[1]), final_norm_w,
                TM_FFN)
```

```python
import functools
import math

import jax
import jax.numpy as jnp
from jax import lax
from jax.experimental import pallas as pl
from jax.experimental.pallas import tpu as pltpu

F32 = jnp.float32
BF16 = jnp.bfloat16

D_MODEL = 1024
DEPTH = 2
GRID_W = 64
HEAD_DIM = 64
ROPE_THETA = 10000.0
EPS = 1e-6
NEG_INF = -1e30
WINDOW = 128
A_WIDTH = D_MODEL // 2
B_HEADS = D_MODEL // 256
B_V_DIM = 2 * HEAD_DIM
C_HEADS = D_MODEL // 128
C_KV_HEADS = C_HEADS // 4
C_Q_WIDTH = C_HEADS * HEAD_DIM
C_KV_WIDTH = C_KV_HEADS * HEAD_DIM
POOL_SIZES = (2, 4, 8, 16)
D_WIDTH = D_MODEL // 2
POOL_GROUP = D_WIDTH // len(POOL_SIZES)
D_FF = ((8 * D_MODEL // 3 + 127) // 128) * 128

LANES = 128
HALO = 8
LOG2E = 1.4426950408889634
QSCALE = (HEAD_DIM ** -0.5) * LOG2E
VMEM_LIMIT = 56 * 1024 * 1024

TM_PROJ = 512
TQ_ATT = 256
TM_FFN = 512
FF_CHUNK = 256
KEY_CHUNK = 512


def _lambda_init(layer):
    return 0.8 - 0.6 * math.exp(-0.3 * layer)


def _cparams(n_axes):
    return pltpu.CompilerParams(dimension_semantics=("arbitrary",) * n_axes,
                                vmem_limit_bytes=VMEM_LIMIT)


def _const_spec(shape):
    nd = len(shape)
    return pl.BlockSpec(shape, lambda *_: (0,) * nd, pipeline_mode=pl.Buffered(1))


def _layer_spec(shape, layer):
    nd = len(shape)
    return pl.BlockSpec((1,) + tuple(shape[1:]), lambda *_: (layer,) + (0,) * (nd - 1),
                        pipeline_mode=pl.Buffered(1))


def _norm_mod(x, shift, scale):
    ms = jnp.mean(x * x, axis=-1, keepdims=True)
    return x * lax.rsqrt(ms + EPS) * (1.0 + scale) + shift


def _sigmoid(x):
    return 1.0 / (1.0 + jnp.exp(-x))


def _rope(x, c, s1, s2):
    outs = []
    for j in range(x.shape[1] // LANES):
        xc = x[:, j * LANES:(j + 1) * LANES]
        outs.append(xc * c + pltpu.roll(xc, LANES - 16, 1) * s1 + pltpu.roll(xc, 16, 1) * s2)
    return outs[0] if len(outs) == 1 else jnp.concatenate(outs, axis=1)


def _rope_tables(n):
    t = jnp.arange(n, dtype=jnp.int32)
    row = (t // GRID_W).astype(F32)
    col = (t % GRID_W).astype(F32)
    axis_dim = HEAD_DIM // 2
    inv = 1.0 / (ROPE_THETA ** (jnp.arange(0, axis_dim, 2, dtype=F32) / axis_dim))
    ar = row[:, None] * inv
    ac = col[:, None] * inv
    cr, sr, cc, sc = jnp.cos(ar), jnp.sin(ar), jnp.cos(ac), jnp.sin(ac)
    z = jnp.zeros_like(cr)
    c64 = jnp.concatenate([cr, cr, cc, cc], axis=-1)
    s1 = jnp.concatenate([-sr, z, -sc, z], axis=-1)
    s2 = jnp.concatenate([z, sr, z, sc], axis=-1)
    reps = LANES // HEAD_DIM
    return tuple(jnp.tile(a, (1, reps)) for a in (c64, s1, s2))


def _mods_kernel(c_ref, w_ref, b_ref, o_ref):
    cv = c_ref[...]
    act = (cv * _sigmoid(cv)).astype(BF16)
    o_ref[0] = jnp.dot(act, w_ref[0].astype(BF16), preferred_element_type=F32) + b_ref[0]


def _mods(cvec, w_mod, b_mod):
    rows = cvec.shape[0]
    nblk = 6
    return pl.pallas_call(
        _mods_kernel,
        grid=(DEPTH, nblk),
        in_specs=[
            pl.BlockSpec((rows, D_MODEL), lambda l, j: (0, 0)),
            pl.BlockSpec((1, D_MODEL, D_MODEL), lambda l, j: (l, 0, j)),
            pl.BlockSpec((1, 1, D_MODEL), lambda l, j: (l, 0, j)),
        ],
        out_specs=pl.BlockSpec((1, rows, D_MODEL), lambda l, j: (l, 0, j)),
        out_shape=jax.ShapeDtypeStruct((DEPTH, rows, 6 * D_MODEL), F32),
        compiler_params=_cparams(2),
        name="mods",
    )(cvec, w_mod, b_mod.reshape(DEPTH, 1, 6 * D_MODEL))


def _proj_ab_kernel(*refs, rope):
    if rope:
        x_ref, sh_ref, sc_ref, w_ref, c_ref, s1_ref, s2_ref, gb_ref, p_ref, q_ref, kt_ref, v_ref = refs
    else:
        x_ref, sh_ref, sc_ref, w_ref, gb_ref, p_ref, q_ref, kt_ref, v_ref = refs
    h = _norm_mod(x_ref[0], sh_ref[0], sc_ref[0]).astype(BF16)

    def proj(j):
        return jnp.dot(h, w_ref[:, j * A_WIDTH:(j + 1) * A_WIDTH], preferred_element_type=F32)

    gb_ref[0] = proj(0)
    p_ref[0] = proj(1) * proj(2)
    q = proj(3)
    k = proj(4)
    if rope:
        c, s1, s2 = c_ref[...], s1_ref[...], s2_ref[...]
        q = _rope(q, c, s1, s2)
        k = _rope(k, c, s1, s2)
    q = q * QSCALE
    first_map = (lax.broadcasted_iota(jnp.int32, q.shape, 1) & HEAD_DIM) == 0
    q_ref[0, 0] = jnp.where(first_map, q, 0.0).astype(BF16)
    q_ref[0, 1] = jnp.where(first_map, 0.0, q).astype(BF16)
    kt_ref[0] = k.T.astype(BF16)
    v = proj(5).astype(BF16)
    ones = jnp.ones((v.shape[0], LANES), BF16)
    pieces = []
    for hd in range(B_HEADS):
        pieces += [v[:, hd * B_V_DIM:(hd + 1) * B_V_DIM], ones]
    v_ref[0] = jnp.concatenate(pieces, axis=1)


def _proj_ab(x, shift, scale, w, tables, tm):
    b, s, _ = x.shape
    nt = s // tm
    rope = tables is not None
    per_batch = shift.shape[0] > 1
    mod_spec = pl.BlockSpec((1, 1, D_MODEL), (lambda i, bb: (bb, 0, 0)) if per_batch else (lambda i, bb: (0, 0, 0)))
    in_specs = [pl.BlockSpec((1, tm, D_MODEL), lambda i, bb: (bb, i, 0)), mod_spec, mod_spec,
                _const_spec(w.shape)]
    args = [x, shift, scale, w]
    if rope:
        in_specs += [pl.BlockSpec((tm, LANES), lambda i, bb: (i, 0))] * 3
        args += list(tables)
    out_shape = (
        jax.ShapeDtypeStruct((b, s, A_WIDTH), F32),
        jax.ShapeDtypeStruct((b, s, A_WIDTH), F32),
        jax.ShapeDtypeStruct((b, 2, s, A_WIDTH), BF16),
        jax.ShapeDtypeStruct((b, A_WIDTH, s), BF16),
        jax.ShapeDtypeStruct((b, s, 2 * A_WIDTH), BF16),
    )
    out_specs = (
        pl.BlockSpec((1, tm, A_WIDTH), lambda i, bb: (bb, i, 0)),
        pl.BlockSpec((1, tm, A_WIDTH), lambda i, bb: (bb, i, 0)),
        pl.BlockSpec((1, 2, tm, A_WIDTH), lambda i, bb: (bb, 0, i, 0)),
        pl.BlockSpec((1, A_WIDTH, tm), lambda i, bb: (bb, 0, i)),
        pl.BlockSpec((1, tm, 2 * A_WIDTH), lambda i, bb: (bb, i, 0)),
    )
    return pl.pallas_call(
        functools.partial(_proj_ab_kernel, rope=rope),
        grid=(nt, b), in_specs=in_specs, out_specs=out_specs, out_shape=out_shape,
        compiler_params=_cparams(2), name="proj_ab_rope" if rope else "proj_ab",
    )(*args)


def _halo_conv3(pm, prev_row, next_row, w):
    rows = pm.shape[0]
    ridx = lax.broadcasted_iota(jnp.int32, pm.shape, 0)
    dn = jnp.where(ridx == 0, prev_row, pltpu.roll(pm, 1, 0))
    up = jnp.where(ridx == rows - 1, next_row, pltpu.roll(pm, rows - 1, 0))
    return dn * w[0:1] + pm * w[1:2] + up * w[2:3]


def _mix_ab_kernel(*refs, has_lat, lam_init):
    if has_lat:
        (q_ref, ktl_ref, vl_ref, ktc_ref, vc_ref, gb_ref, p_ref, pp_ref, pn_ref, cw_ref, lq_ref, sl_ref,
         wo_ref, x_ref, g_ref, o_ref) = refs
        keys = ((ktl_ref, vl_ref), (ktc_ref, vc_ref))
    else:
        (q_ref, ktc_ref, vc_ref, gb_ref, p_ref, pp_ref, pn_ref, cw_ref, lq_ref, sl_ref,
         wo_ref, x_ref, g_ref, o_ref) = refs
        keys = ((ktc_ref, vc_ref),)
    i = pl.program_id(1)
    last = pl.num_programs(1) - 1
    tq = x_ref.shape[1]

    lq = lq_ref[...]
    lam = (jnp.exp(jnp.sum(lq[0:1] * lq[1:2], axis=-1, keepdims=True))
           - jnp.exp(jnp.sum(lq[2:3] * lq[3:4], axis=-1, keepdims=True)) + lam_init)

    prev_row = jnp.where(i > 0, pp_ref[0, HALO - 1:HALO, :], 0.0)
    next_row = jnp.where(i < last, pn_ref[0, 0:1, :], 0.0)
    ya = gb_ref[0] * _halo_conv3(p_ref[0], prev_row, next_row, cw_ref[...])
    y = jnp.dot(ya.astype(BF16), wo_ref[0:A_WIDTH, :], preferred_element_type=F32)

    chunks = []
    for hd in range(B_HEADS):
        for kt, vr in keys:
            nk = kt.shape[2]
            for k0 in range(0, nk, KEY_CHUNK):
                chunks.append((hd, kt, vr, k0, min(k0 + KEY_CHUNK, nk)))

    def scores(ci):
        hd, kt, _, k0, k1 = chunks[ci]
        lo, hi = hd * B_V_DIM, (hd + 1) * B_V_DIM
        qs = q_ref[0, :, :, lo:hi].reshape(2 * tq, B_V_DIM)
        return jnp.dot(qs, kt[0, lo:hi, k0:k1], preferred_element_type=F32)

    def finish_head(hd, parts):
        m = functools.reduce(jnp.maximum, [mc for mc, _ in parts])
        o = None
        for mc, oc in parts:
            t = jnp.exp2(mc - m) * oc
            o = t if o is None else o + t
        o1 = o[:tq, :B_V_DIM] / o[:tq, B_V_DIM:B_V_DIM + 1]
        o2 = o[tq:, :B_V_DIM] / o[tq:, B_V_DIM:B_V_DIM + 1]
        oh = o1 - lam * o2
        ms = jnp.mean(oh * oh, axis=-1, keepdims=True)
        return oh * lax.rsqrt(ms + EPS) * sl_ref[...] * (1.0 - lam_init)

    yb = []
    parts = []
    sc_next = scores(0)
    for ci, (hd, _, vr, k0, k1) in enumerate(chunks):
        sc = sc_next
        if ci + 1 < len(chunks):
            sc_next = scores(ci + 1)
        mc = jnp.max(sc, axis=-1, keepdims=True)
        pc = jnp.exp2(sc - mc).astype(BF16)
        parts.append((mc, jnp.dot(pc, vr[0, k0:k1, 2 * hd * B_V_DIM:2 * (hd + 1) * B_V_DIM],
                                  preferred_element_type=F32)))
        if ci + 1 == len(chunks) or chunks[ci + 1][0] != hd:
            yb.append(finish_head(hd, parts))
            parts = []
            if hd % 2 == 1:
                w_lo = A_WIDTH + (hd - 1) * B_V_DIM
                pair = jnp.concatenate(yb[-2:], axis=1).astype(BF16)
                y = y + jnp.dot(pair, wo_ref[w_lo:w_lo + 2 * B_V_DIM, :], preferred_element_type=F32)

    o_ref[0] = x_ref[0] + g_ref[0] * y


def _mix_ab(q2, lat_kv, ctx_kv, gb, p, conv_w, lam_qk, subln_w, w_out, x, gate, lam_init, tq):
    b, s, _ = x.shape
    nq = s // tq
    nh = s // HALO
    per_batch = gate.shape[0] > 1
    has_lat = lat_kv is not None
    bmap = lambda bb, i: (bb, 0, 0)
    in_specs = [pl.BlockSpec((1, 2, tq, A_WIDTH), lambda bb, i: (bb, 0, i, 0))]
    args = [q2]
    for kv in ([lat_kv] if has_lat else []) + [ctx_kv]:
        kt, vx = kv
        in_specs += [pl.BlockSpec((1,) + kt.shape[1:], bmap), pl.BlockSpec((1,) + vx.shape[1:], bmap)]
        args += [kt, vx]
    tile = pl.BlockSpec((1, tq, A_WIDTH), lambda bb, i: (bb, i, 0))
    in_specs += [
        tile, tile,
        pl.BlockSpec((1, HALO, A_WIDTH), lambda bb, i: (bb, jnp.maximum(i * (tq // HALO) - 1, 0), 0)),
        pl.BlockSpec((1, HALO, A_WIDTH), lambda bb, i: (bb, jnp.minimum((i + 1) * (tq // HALO), nh - 1), 0)),
        _const_spec(conv_w.shape), _const_spec(lam_qk.shape), _const_spec((1, B_V_DIM)),
        _const_spec(w_out.shape),
        pl.BlockSpec((1, tq, D_MODEL), lambda bb, i: (bb, i, 0)),
        pl.BlockSpec((1, 1, D_MODEL), bmap if per_batch else (lambda bb, i: (0, 0, 0))),
    ]
    args += [gb, p, p, p, conv_w, lam_qk, subln_w.reshape(1, B_V_DIM), w_out, x, gate]
    return pl.pallas_call(
        functools.partial(_mix_ab_kernel, has_lat=has_lat, lam_init=lam_init),
        grid=(b, nq), in_specs=in_specs,
        out_specs=pl.BlockSpec((1, tq, D_MODEL), lambda bb, i: (bb, i, 0)),
        out_shape=jax.ShapeDtypeStruct(x.shape, F32),
        compiler_params=_cparams(2), name="mix_ab_lat" if has_lat else "mix_ab_ctx",
    )(*args)


def _ffn_kernel(*refs, final_norm):
    if final_norm:
        (x_ref, xp_ref, xn_ref, sh_ref, sc_ref, g_ref, wu_ref, cw_ref, cb_ref, wd_ref, fw_ref,
         o_ref, act_scr) = refs
    else:
        (x_ref, xp_ref, xn_ref, sh_ref, sc_ref, g_ref, wu_ref, cw_ref, cb_ref, wd_ref,
         o_ref, act_scr) = refs
    i = pl.program_id(1)
    last = pl.num_programs(1) - 1
    tm = x_ref.shape[1]
    rows = tm + 2 * HALO
    sh, sc = sh_ref[0], sc_ref[0]
    x = x_ref[0]
    hp = jnp.where(i > 0, _norm_mod(xp_ref[0], sh, sc), 0.0)
    hn = jnp.where(i < last, _norm_mod(xn_ref[0], sh, sc), 0.0)
    hext = jnp.concatenate([hp, _norm_mod(x, sh, sc), hn], axis=0).astype(BF16)

    for c in range(D_FF // FF_CHUNK):
        halves = []
        for part in range(2):
            col = part * D_FF + c * FF_CHUNK
            u = jnp.dot(hext, wu_ref[0, :, col:col + FF_CHUNK], preferred_element_type=F32)
            cw = cw_ref[:, col:col + FF_CHUNK]
            halves.append(pltpu.roll(u, 1, 0)[HALO:HALO + tm] * cw[0:1]
                          + u[HALO:HALO + tm] * cw[1:2]
                          + pltpu.roll(u, rows - 1, 0)[HALO:HALO + tm] * cw[2:3]
                          + cb_ref[:, col:col + FF_CHUNK])
        a, g = halves
        act_scr[:, c * FF_CHUNK:(c + 1) * FF_CHUNK] = (g * _sigmoid(g) * a).astype(BF16)

    y = jnp.dot(act_scr[...], wd_ref[0], preferred_element_type=F32)
    out = x + g_ref[0] * y
    if final_norm:
        ms = jnp.mean(out * out, axis=-1, keepdims=True)
        out = out * lax.rsqrt(ms + EPS) * fw_ref[...]
    o_ref[0] = out


def _ffn(x, shift, scale, gate, layer, w_up, conv_w, conv_b, w_down, final_w, tm):
    b, s, _ = x.shape
    nt = s // tm
    nh = s // HALO
    per_batch = shift.shape[0] > 1
    final_norm = final_w is not None
    mod_spec = pl.BlockSpec((1, 1, D_MODEL), (lambda bb, i: (bb, 0, 0)) if per_batch else (lambda bb, i: (0, 0, 0)))
    in_specs = [
        pl.BlockSpec((1, tm, D_MODEL), lambda bb, i: (bb, i, 0)),
        pl.BlockSpec((1, HALO, D_MODEL), lambda bb, i: (bb, jnp.maximum(i * (tm // HALO) - 1, 0), 0)),
        pl.BlockSpec((1, HALO, D_MODEL), lambda bb, i: (bb, jnp.minimum((i + 1) * (tm // HALO), nh - 1), 0)),
        mod_spec, mod_spec, mod_spec,
        _layer_spec(w_up.shape, layer), _const_spec(conv_w.shape), _const_spec((1, 2 * D_FF)),
        _layer_spec(w_down.shape, layer),
    ]
    args = [x, x, x, shift, scale, gate, w_up, conv_w, conv_b.reshape(1, 2 * D_FF), w_down]
    if final_norm:
        in_specs.append(_const_spec((1, D_MODEL)))
        args.append(final_w.reshape(1, D_MODEL))
    return pl.pallas_call(
        functools.partial(_ffn_kernel, final_norm=final_norm),
        grid=(b, nt), in_specs=in_specs,
        out_specs=pl.BlockSpec((1, tm, D_MODEL), lambda bb, i: (bb, i, 0)),
        out_shape=jax.ShapeDtypeStruct(x.shape, F32),
        scratch_shapes=[pltpu.VMEM((tm, D_FF), BF16)],
        compiler_params=_cparams(2), name="ffn_final" if final_norm else "ffn",
    )(*args)


def _dup_halves(a):
    sw = pltpu.roll(a, HEAD_DIM, 1)
    first = lax.broadcasted_iota(jnp.int32, a.shape, 1) < HEAD_DIM
    return jnp.where(first, a, sw), jnp.where(first, sw, a)


def _proj_cd_kernel(*refs, rope, kv_only):
    if kv_only:
        x_ref, sh_ref, sc_ref, w_ref, kt_ref, v_ref = refs
    else:
        x_ref, sh_ref, sc_ref, w_ref, c_ref, s1_ref, s2_ref, q_ref, kt_ref, v_ref, pl_ref = refs
    h = _norm_mod(x_ref[0], sh_ref[0], sc_ref[0]).astype(BF16)
    kv = jnp.dot(h, w_ref[:, C_Q_WIDTH:C_Q_WIDTH + 2 * C_KV_WIDTH], preferred_element_type=F32)
    k = kv[:, :C_KV_WIDTH]
    v = kv[:, C_KV_WIDTH:]
    if rope:
        c, s1, s2 = c_ref[...], s1_ref[...], s2_ref[...]
        k = _rope(k, c, s1, s2)
    k0, k1 = _dup_halves(k)
    kt_ref[0, 0] = k0.T.astype(BF16)
    kt_ref[0, 1] = k1.T.astype(BF16)
    v0, v1 = _dup_halves(v)
    ones = jnp.ones(v.shape, BF16)
    v_ref[0] = jnp.concatenate([v0.astype(BF16), ones, v1.astype(BF16), ones], axis=1)
    if not kv_only:
        q = jnp.dot(h, w_ref[:, :C_Q_WIDTH], preferred_element_type=F32)
        if rope:
            q = _rope(q, c, s1, s2)
        q_ref[0] = (q * QSCALE).astype(BF16)
        pl_ref[0] = jnp.dot(h, w_ref[:, C_Q_WIDTH + 2 * C_KV_WIDTH:], preferred_element_type=F32)


def _proj_cd(x, shift, scale, w, tables, tm):
    b, s, _ = x.shape
    nt = s // tm
    kv_only = tables is None
    per_batch = shift.shape[0] > 1
    mod_spec = pl.BlockSpec((1, 1, D_MODEL), (lambda i, bb: (bb, 0, 0)) if per_batch else (lambda i, bb: (0, 0, 0)))
    in_specs = [pl.BlockSpec((1, tm, D_MODEL), lambda i, bb: (bb, i, 0)), mod_spec, mod_spec,
                _const_spec(w.shape)]
    args = [x, shift, scale, w]
    kv_shapes = (
        jax.ShapeDtypeStruct((b, C_KV_HEADS, LANES, s), BF16),
        jax.ShapeDtypeStruct((b, s, 2 * C_KV_HEADS * LANES), BF16),
    )
    kv_specs = (
        pl.BlockSpec((1, C_KV_HEADS, LANES, tm), lambda i, bb: (bb, 0, 0, i)),
        pl.BlockSpec((1, tm, 2 * C_KV_HEADS * LANES), lambda i, bb: (bb, i, 0)),
    )
    if kv_only:
        out_shape, out_specs = kv_shapes, kv_specs
    else:
        in_specs += [pl.BlockSpec((tm, LANES), lambda i, bb: (i, 0))] * 3
        args += list(tables)
        out_shape = (jax.ShapeDtypeStruct((b, s, C_Q_WIDTH), BF16),) + kv_shapes + (
            jax.ShapeDtypeStruct((b, s, D_WIDTH), F32),)
        out_specs = (pl.BlockSpec((1, tm, C_Q_WIDTH), lambda i, bb: (bb, i, 0)),) + kv_specs + (
            pl.BlockSpec((1, tm, D_WIDTH), lambda i, bb: (bb, i, 0)),)
    return pl.pallas_call(
        functools.partial(_proj_cd_kernel, rope=not kv_only, kv_only=kv_only),
        grid=(nt, b), in_specs=in_specs, out_specs=out_specs, out_shape=out_shape,
        compiler_params=_cparams(2), name="proj_cd_kv" if kv_only else "proj_cd",
    )(*args)


def _mix_cd_kernel(q_ref, ktp_ref, ktm_ref, ktn_ref, ktc_ref, vp_ref, vm_ref, vn_ref, vc_ref,
                   pl_ref, plp_ref, pln_ref, sink_ref, pw_ref, ps_ref, wo_ref, x_ref, g_ref, o_ref, pool_scr):
    i = pl.program_id(1)
    last = pl.num_programs(1) - 1
    tq = x_ref.shape[1]
    group = C_HEADS // C_KV_HEADS

    t = lax.broadcasted_iota(jnp.int32, (tq, tq + 2 * WINDOW), 0)
    cidx = lax.broadcasted_iota(jnp.int32, (tq, tq + 2 * WINDOW), 1)
    rel = cidx - WINDOW - t
    c_lo = jnp.where(i > 0, 0, WINDOW)
    c_hi = jnp.where(i < last, tq + 2 * WINDOW, tq + WINDOW)
    bias = jnp.where(jnp.abs(rel) <= WINDOW, 0.0, NEG_INF)
    bias = jnp.where(cidx >= c_lo, bias, NEG_INF)
    bias = jnp.where(cidx < c_hi, bias, NEG_INF)
    bias = jnp.concatenate([bias] * group, axis=0)

    q = q_ref[0]
    first = (lax.broadcasted_iota(jnp.int32, (tq, LANES), 1) < HEAD_DIM)
    zero = jnp.zeros((tq, LANES), BF16)

    ya = []
    for j in range(C_KV_HEADS):
        rows = []
        for cc in range(group // 2):
            qc = q[:, (j * (group // 2) + cc) * LANES:(j * (group // 2) + cc + 1) * LANES]
            rows += [jnp.where(first, qc, zero), jnp.where(first, zero, qc)]
        qs = jnp.concatenate(rows, axis=0)
        s_c = jnp.dot(qs, ktc_ref[0, j], preferred_element_type=F32)
        s_w = jnp.concatenate(
            [jnp.dot(qs, kr[0, j], preferred_element_type=F32) for kr in (ktp_ref, ktm_ref, ktn_ref)],
            axis=1) + bias
        sink = jnp.concatenate(
            [jnp.broadcast_to(sink_ref[:, j * group + r:j * group + r + 1] * LOG2E, (tq, 1)) for r in range(group)],
            axis=0)
        m = jnp.maximum(jnp.maximum(jnp.max(s_c, axis=-1, keepdims=True), jnp.max(s_w, axis=-1, keepdims=True)),
                        sink)
        p_c = jnp.exp2(s_c - m).astype(BF16)
        p_w = jnp.exp2(s_w - m).astype(BF16)
        vlo, vhi = j * 2 * LANES, (j + 1) * 2 * LANES
        o = jnp.dot(p_c, vc_ref[0, :, vlo:vhi], preferred_element_type=F32)
        o += jnp.dot(p_w[:, :WINDOW], vp_ref[0, :, vlo:vhi], preferred_element_type=F32)
        o += jnp.dot(p_w[:, WINDOW:WINDOW + tq], vm_ref[0, :, vlo:vhi], preferred_element_type=F32)
        o += jnp.dot(p_w[:, WINDOW + tq:], vn_ref[0, :, vlo:vhi], preferred_element_type=F32)
        den = o[:, LANES:LANES + 1] + jnp.exp2(sink - m)
        oh = o[:, :LANES] / den
        for cc in range(group // 2):
            ya.append(jnp.where(first, oh[(2 * cc) * tq:(2 * cc + 1) * tq], oh[(2 * cc + 1) * tq:(2 * cc + 2) * tq]))

    pm = pl_ref[0]
    pool_scr[0:HALO, :] = jnp.where(i > 0, plp_ref[0], 0.0)
    pool_scr[HALO:HALO + tq, :] = pm
    pool_scr[HALO + tq:, :] = jnp.where(i < last, pln_ref[0], 0.0)
    n = tq * pl.num_programs(1)
    pos = i * tq + lax.broadcasted_iota(jnp.int32, (tq, POOL_GROUP), 0)
    yd = []
    for gi, w in enumerate(POOL_SIZES):
        glo, ghi = gi * POOL_GROUP, (gi + 1) * POOL_GROUP
        acc = None
        for d in range(-(w // 2), w // 2):
            term = pool_scr[HALO + d:HALO + d + tq, glo:ghi]
            acc = term if acc is None else acc + term
        cnt = (jnp.minimum(pos + w // 2, n) - jnp.maximum(pos - w // 2, 0)).astype(F32)
        z = (acc / cnt - pm[:, glo:ghi]).astype(BF16)
        yd.append(jnp.dot(z, pw_ref[gi], preferred_element_type=F32) * ps_ref[:, glo:ghi])

    y = jnp.concatenate(ya + yd, axis=1).astype(BF16)
    y = jnp.dot(y, wo_ref[...], preferred_element_type=F32)
    o_ref[0] = x_ref[0] + g_ref[0] * y


def _mix_cd(q, kt, vx, ktc, vxc, plv, sink, pool_w, pool_scale, w_out, x, gate, tq):
    b, s, _ = x.shape
    nq = s // tq
    nh = s // HALO
    nw = s // WINDOW
    r = tq // WINDOW
    lc = ktc.shape[-1]
    vw = vx.shape[-1]
    prev_w = lambda bb, i: jnp.maximum(i * r - 1, 0)
    next_w = lambda bb, i: jnp.minimum((i + 1) * r, nw - 1)
    in_specs = [
        pl.BlockSpec((1, tq, C_Q_WIDTH), lambda bb, i: (bb, i, 0)),
        pl.BlockSpec((1, C_KV_HEADS, LANES, WINDOW), lambda bb, i: (bb, 0, 0, prev_w(bb, i))),
        pl.BlockSpec((1, C_KV_HEADS, LANES, tq), lambda bb, i: (bb, 0, 0, i)),
        pl.BlockSpec((1, C_KV_HEADS, LANES, WINDOW), lambda bb, i: (bb, 0, 0, next_w(bb, i))),
        pl.BlockSpec((1, C_KV_HEADS, LANES, lc), lambda bb, i: (bb, 0, 0, 0)),
        pl.BlockSpec((1, WINDOW, vw), lambda bb, i: (bb, prev_w(bb, i), 0)),
        pl.BlockSpec((1, tq, vw), lambda bb, i: (bb, i, 0)),
        pl.BlockSpec((1, WINDOW, vw), lambda bb, i: (bb, next_w(bb, i), 0)),
        pl.BlockSpec((1, lc, vw), lambda bb, i: (bb, 0, 0)),
        pl.BlockSpec((1, tq, D_WIDTH), lambda bb, i: (bb, i, 0)),
        pl.BlockSpec((1, HALO, D_WIDTH), lambda bb, i: (bb, jnp.maximum(i * (tq // HALO) - 1, 0), 0)),
        pl.BlockSpec((1, HALO, D_WIDTH), lambda bb, i: (bb, jnp.minimum((i + 1) * (tq // HALO), nh - 1), 0)),
        _const_spec((1, C_HEADS)), _const_spec(pool_w.shape), _const_spec((1, D_WIDTH)), _const_spec(w_out.shape),
        pl.BlockSpec((1, tq, D_MODEL), lambda bb, i: (bb, i, 0)),
        pl.BlockSpec((1, 1, D_MODEL), lambda bb, i: (bb, 0, 0)),
    ]
    args = [q, kt, kt, kt, ktc, vx, vx, vx, vxc, plv, plv, plv, sink.reshape(1, C_HEADS), pool_w,
            pool_scale.reshape(1, D_WIDTH), w_out, x, gate]
    return pl.pallas_call(
        _mix_cd_kernel,
        grid=(b, nq), in_specs=in_specs,
        out_specs=pl.BlockSpec((1, tq, D_MODEL), lambda bb, i: (bb, i, 0)),
        out_shape=jax.ShapeDtypeStruct(x.shape, F32),
        scratch_shapes=[pltpu.VMEM((tq + 2 * HALO, D_WIDTH), F32)],
        compiler_params=_cparams(2), name="mix_cd",
    )(*args)


def kernel(x, c, ctx, c_ctx, w_mod, b_mod, w_in_ab, conv_a, lam_qk, subln_b, w_out_ab, w_in_cd, sink_c, pool_w,
           pool_scale, w_out_cd, w_up, conv_ffn_w, conv_ffn_b, w_down, final_norm_w):
    b, s, d = x.shape
    lc = ctx.shape[1]
    assert d == D_MODEL and DEPTH == 2 and s % TM_PROJ == 0 and s % TM_FFN == 0 and s % TQ_ATT == 0

    rows = ((b + 1 + HALO - 1) // HALO) * HALO
    cvec = jnp.concatenate([c, c_ctx[None, :], jnp.zeros((rows - b - 1, d), F32)], axis=0)
    mods = _mods(cvec, w_mod, b_mod)

    def split(l):
        ml = [mods[l, :b, j * d:(j + 1) * d].reshape(b, 1, d) for j in range(6)]
        mc = [mods[l, b:b + 1, j * d:(j + 1) * d].reshape(1, 1, d) for j in range(6)]
        return ml, mc

    tables = _rope_tables(s)
    bf = lambda a: a.astype(BF16)

    ml, mc = split(0)
    w_in = bf(w_in_ab[0])
    w_out = bf(w_out_ab[0])
    gb, p, q2, kt, vx = _proj_ab(x, ml[0], ml[1], w_in, tables, TM_PROJ)
    gbc, pc, q2c, ktc, vxc = _proj_ab(ctx, mc[0], mc[1], w_in, None, lc)
    lam0 = _lambda_init(0)
    xl = _mix_ab(q2, (kt, vx), (ktc, vxc), gb, p, conv_a[0], lam_qk[0], subln_b[0], w_out, x, ml[2], lam0, TQ_ATT)
    xc = _mix_ab(q2c, None, (ktc, vxc), gbc, pc, conv_a[0], lam_qk[0], subln_b[0], w_out, ctx, mc[2], lam0, lc)
    w_up_bf, w_down_bf = bf(w_up), bf(w_down)
    ffn_w = (0, w_up_bf, conv_ffn_w[0], conv_ffn_b[0], w_down_bf)
    xl = _ffn(xl, ml[3], ml[4], ml[5], *ffn_w, None, TM_FFN)
    xc = _ffn(xc, mc[3], mc[4], mc[5], *ffn_w, None, lc)

    ml, mc = split(1)
    w_in = bf(w_in_cd[0])
    q, kt, vx, plv = _proj_cd(xl, ml[0], ml[1], w_in, tables, TM_PROJ)
    ktc, vxc = _proj_cd(xc, mc[0], mc[1], w_in, None, lc)
    xl = _mix_cd(q, kt, vx, ktc, vxc, plv, sink_c[0], bf(pool_w[0]), pool_scale[0], bf(w_out_cd[0]), xl, ml[2],
                 TQ_ATT)
    return _ffn(xl, ml[3], ml[4], ml[5], 1, w_up_bf, conv_ffn_w[1], conv_ffn_b[1], w_down_bf, final_norm_w, TM_FFN)
```

```python
import functools
import math

import jax
import jax.numpy as jnp
from jax import lax
from jax.experimental import pallas as pl
from jax.experimental.pallas import tpu as pltpu

F32 = jnp.float32
BF16 = jnp.bfloat16

D_MODEL = 1024
DEPTH = 2
GRID_W = 64
HEAD_DIM = 64
ROPE_THETA = 10000.0
EPS = 1e-6
NEG_INF = -1e30
WINDOW = 128
A_WIDTH = D_MODEL // 2
B_HEADS = D_MODEL // 256
B_V_DIM = 2 * HEAD_DIM
C_HEADS = D_MODEL // 128
C_KV_HEADS = C_HEADS // 4
C_Q_WIDTH = C_HEADS * HEAD_DIM
C_KV_WIDTH = C_KV_HEADS * HEAD_DIM
POOL_SIZES = (2, 4, 8, 16)
D_WIDTH = D_MODEL // 2
POOL_GROUP = D_WIDTH // len(POOL_SIZES)
D_FF = ((8 * D_MODEL // 3 + 127) // 128) * 128

LANES = 128
HALO = 8
LOG2E = 1.4426950408889634
QSCALE = (HEAD_DIM ** -0.5) * LOG2E
VMEM_LIMIT = 56 * 1024 * 1024

TM_PROJ = 512
TQ_AB = 512
TQ_ATT = 256
TM_FFN = 512
FF_CHUNK = 256
KEY_CHUNK = 1024
FINISH_DELAY = 1


def _lambda_init(layer):
    return 0.8 - 0.6 * math.exp(-0.3 * layer)


def _cparams(n_axes):
    return pltpu.CompilerParams(dimension_semantics=("arbitrary",) * n_axes,
                                vmem_limit_bytes=VMEM_LIMIT)


def _const_spec(shape):
    nd = len(shape)
    return pl.BlockSpec(shape, lambda *_: (0,) * nd, pipeline_mode=pl.Buffered(1))


def _layer_spec(shape, layer):
    nd = len(shape)
    return pl.BlockSpec((1,) + tuple(shape[1:]), lambda *_: (layer,) + (0,) * (nd - 1),
                        pipeline_mode=pl.Buffered(1))


def _norm_mod(x, shift, scale):
    ms = jnp.mean(x * x, axis=-1, keepdims=True)
    return x * lax.rsqrt(ms + EPS) * (1.0 + scale) + shift


def _sigmoid(x):
    return 1.0 / (1.0 + jnp.exp(-x))


def _rope(x, c, s1, s2):
    outs = []
    for j in range(x.shape[1] // LANES):
        xc = x[:, j * LANES:(j + 1) * LANES]
        outs.append(xc * c + pltpu.roll(xc, LANES - 16, 1) * s1 + pltpu.roll(xc, 16, 1) * s2)
    return outs[0] if len(outs) == 1 else jnp.concatenate(outs, axis=1)


def _rope_tables(n):
    t = jnp.arange(n, dtype=jnp.int32)
    row = (t // GRID_W).astype(F32)
    col = (t % GRID_W).astype(F32)
    axis_dim = HEAD_DIM // 2
    inv = 1.0 / (ROPE_THETA ** (jnp.arange(0, axis_dim, 2, dtype=F32) / axis_dim))
    ar = row[:, None] * inv
    ac = col[:, None] * inv
    cr, sr, cc, sc = jnp.cos(ar), jnp.sin(ar), jnp.cos(ac), jnp.sin(ac)
    z = jnp.zeros_like(cr)
    c64 = jnp.concatenate([cr, cr, cc, cc], axis=-1)
    s1 = jnp.concatenate([-sr, z, -sc, z], axis=-1)
    s2 = jnp.concatenate([z, sr, z, sc], axis=-1)
    reps = LANES // HEAD_DIM
    return tuple(jnp.tile(a, (1, reps)) for a in (c64, s1, s2))


def _mods_kernel(c_ref, w_ref, b_ref, o_ref):
    cv = c_ref[...]
    act = (cv * _sigmoid(cv)).astype(BF16)
    o_ref[0] = jnp.dot(act, w_ref[0].astype(BF16), preferred_element_type=F32) + b_ref[0]


def _mods(cvec, w_mod, b_mod):
    rows = cvec.shape[0]
    nblk = 6
    return pl.pallas_call(
        _mods_kernel,
        grid=(DEPTH, nblk),
        in_specs=[
            pl.BlockSpec((rows, D_MODEL), lambda l, j: (0, 0)),
            pl.BlockSpec((1, D_MODEL, D_MODEL), lambda l, j: (l, 0, j)),
            pl.BlockSpec((1, 1, D_MODEL), lambda l, j: (l, 0, j)),
        ],
        out_specs=pl.BlockSpec((1, rows, D_MODEL), lambda l, j: (l, 0, j)),
        out_shape=jax.ShapeDtypeStruct((DEPTH, rows, 6 * D_MODEL), F32),
        compiler_params=_cparams(2),
        name="mods",
    )(cvec, w_mod, b_mod.reshape(DEPTH, 1, 6 * D_MODEL))


def _proj_ab_kernel(*refs, rope):
    if rope:
        x_ref, sh_ref, sc_ref, w_ref, c_ref, s1_ref, s2_ref, gb_ref, p_ref, q_ref, kt_ref, v_ref = refs
    else:
        x_ref, sh_ref, sc_ref, w_ref, gb_ref, p_ref, q_ref, kt_ref, v_ref = refs
    h = _norm_mod(x_ref[0], sh_ref[0], sc_ref[0]).astype(BF16)

    def proj(j):
        return jnp.dot(h, w_ref[:, j * A_WIDTH:(j + 1) * A_WIDTH], preferred_element_type=F32)

    gb_ref[0] = proj(0)
    p_ref[0] = proj(1) * proj(2)
    q = proj(3)
    k = proj(4)
    if rope:
        c, s1, s2 = c_ref[...], s1_ref[...], s2_ref[...]
        q = _rope(q, c, s1, s2)
        k = _rope(k, c, s1, s2)
    q = q * QSCALE
    first_map = (lax.broadcasted_iota(jnp.int32, q.shape, 1) & HEAD_DIM) == 0
    q_ref[0, 0] = jnp.where(first_map, q, 0.0).astype(BF16)
    q_ref[0, 1] = jnp.where(first_map, 0.0, q).astype(BF16)
    kt_ref[0] = k.T.astype(BF16)
    v = proj(5).astype(BF16)
    ones = jnp.ones((v.shape[0], LANES), BF16)
    pieces = []
    for hd in range(B_HEADS):
        pieces += [v[:, hd * B_V_DIM:(hd + 1) * B_V_DIM], ones]
    v_ref[0] = jnp.concatenate(pieces, axis=1)


def _proj_ab(x, shift, scale, w, tables, tm):
    b, s, _ = x.shape
    nt = s // tm
    rope = tables is not None
    per_batch = shift.shape[0] > 1
    mod_spec = pl.BlockSpec((1, 1, D_MODEL), (lambda i, bb: (bb, 0, 0)) if per_batch else (lambda i, bb: (0, 0, 0)))
    in_specs = [pl.BlockSpec((1, tm, D_MODEL), lambda i, bb: (bb, i, 0)), mod_spec, mod_spec,
                _const_spec(w.shape)]
    args = [x, shift, scale, w]
    if rope:
        in_specs += [pl.BlockSpec((tm, LANES), lambda i, bb: (i, 0))] * 3
        args += list(tables)
    out_shape = (
        jax.ShapeDtypeStruct((b, s, A_WIDTH), F32),
        jax.ShapeDtypeStruct((b, s, A_WIDTH), F32),
        jax.ShapeDtypeStruct((b, 2, s, A_WIDTH), BF16),
        jax.ShapeDtypeStruct((b, A_WIDTH, s), BF16),
        jax.ShapeDtypeStruct((b, s, 2 * A_WIDTH), BF16),
    )
    out_specs = (
        pl.BlockSpec((1, tm, A_WIDTH), lambda i, bb: (bb, i, 0)),
        pl.BlockSpec((1, tm, A_WIDTH), lambda i, bb: (bb, i, 0)),
        pl.BlockSpec((1, 2, tm, A_WIDTH), lambda i, bb: (bb, 0, i, 0)),
        pl.BlockSpec((1, A_WIDTH, tm), lambda i, bb: (bb, 0, i)),
        pl.BlockSpec((1, tm, 2 * A_WIDTH), lambda i, bb: (bb, i, 0)),
    )
    return pl.pallas_call(
        functools.partial(_proj_ab_kernel, rope=rope),
        grid=(nt, b), in_specs=in_specs, out_specs=out_specs, out_shape=out_shape,
        compiler_params=_cparams(2), name="proj_ab_rope" if rope else "proj_ab",
    )(*args)


def _halo_conv3(pm, prev_row, next_row, w):
    rows = pm.shape[0]
    ridx = lax.broadcasted_iota(jnp.int32, pm.shape, 0)
    dn = jnp.where(ridx == 0, prev_row, pltpu.roll(pm, 1, 0))
    up = jnp.where(ridx == rows - 1, next_row, pltpu.roll(pm, rows - 1, 0))
    return dn * w[0:1] + pm * w[1:2] + up * w[2:3]


def _mix_ab_kernel(*refs, has_lat, lam_init):
    if has_lat:
        (q_ref, ktl_ref, vl_ref, ktc_ref, vc_ref, gb_ref, p_ref, pp_ref, pn_ref, cw_ref, lq_ref, sl_ref,
         wo_ref, x_ref, g_ref, o_ref) = refs
        keys = ((ktl_ref, vl_ref), (ktc_ref, vc_ref))
    else:
        (q_ref, ktc_ref, vc_ref, gb_ref, p_ref, pp_ref, pn_ref, cw_ref, lq_ref, sl_ref,
         wo_ref, x_ref, g_ref, o_ref) = refs
        keys = ((ktc_ref, vc_ref),)
    i = pl.program_id(1)
    last = pl.num_programs(1) - 1
    tq = x_ref.shape[1]

    lq = lq_ref[...]
    lam = (jnp.exp(jnp.sum(lq[0:1] * lq[1:2], axis=-1, keepdims=True))
           - jnp.exp(jnp.sum(lq[2:3] * lq[3:4], axis=-1, keepdims=True)) + lam_init)

    prev_row = jnp.where(i > 0, pp_ref[0, HALO - 1:HALO, :], 0.0)
    next_row = jnp.where(i < last, pn_ref[0, 0:1, :], 0.0)
    ya = gb_ref[0] * _halo_conv3(p_ref[0], prev_row, next_row, cw_ref[...])
    y = jnp.dot(ya.astype(BF16), wo_ref[0:A_WIDTH, :], preferred_element_type=F32)

    chunks = []
    for hd in range(B_HEADS):
        for kt, vr in keys:
            nk = kt.shape[2]
            for k0 in range(0, nk, KEY_CHUNK):
                chunks.append((hd, kt, vr, k0, min(k0 + KEY_CHUNK, nk)))

    def scores(ci):
        hd, kt, _, k0, k1 = chunks[ci]
        lo, hi = hd * B_V_DIM, (hd + 1) * B_V_DIM
        qs = q_ref[0, :, :, lo:hi].reshape(2 * tq, B_V_DIM)
        return jnp.dot(qs, kt[0, lo:hi, k0:k1], preferred_element_type=F32)

    def finish_head(hd, parts):
        m = functools.reduce(jnp.maximum, [mc for mc, _ in parts])
        o = None
        for mc, oc in parts:
            t = jnp.exp2(mc - m) * oc
            o = t if o is None else o + t
        o1 = o[:tq, :B_V_DIM] / o[:tq, B_V_DIM:B_V_DIM + 1]
        o2 = o[tq:, :B_V_DIM] / o[tq:, B_V_DIM:B_V_DIM + 1]
        oh = o1 - lam * o2
        ms = jnp.mean(oh * oh, axis=-1, keepdims=True)
        return oh * lax.rsqrt(ms + EPS) * sl_ref[...] * (1.0 - lam_init)

    yb = {}
    acc = [y]
    pending = []

    def finish(hd, parts):
        yb[hd] = finish_head(hd, parts)
        if hd % 2 == 1:
            w_lo = A_WIDTH + (hd - 1) * B_V_DIM
            pair = jnp.concatenate([yb[hd - 1], yb[hd]], axis=1).astype(BF16)
            acc[0] = acc[0] + jnp.dot(pair, wo_ref[w_lo:w_lo + 2 * B_V_DIM, :], preferred_element_type=F32)

    parts = []
    sc_next = scores(0)
    for ci, (hd, _, vr, k0, k1) in enumerate(chunks):
        sc = sc_next
        if ci + 1 < len(chunks):
            sc_next = scores(ci + 1)
        mc = jnp.max(sc, axis=-1, keepdims=True)
        pc = jnp.exp2(sc - mc).astype(BF16)
        vh = vr[0, k0:k1, 2 * hd * B_V_DIM:2 * (hd + 1) * B_V_DIM]
        parts.append((mc, jnp.concatenate(
            [jnp.dot(pc[mp * tq:(mp + 1) * tq], vh, preferred_element_type=F32) for mp in range(2)], axis=0)))
        if ci + 1 == len(chunks) or chunks[ci + 1][0] != hd:
            pending.append((ci + FINISH_DELAY, hd, parts))
            parts = []
        while pending and pending[0][0] <= ci:
            finish(*pending.pop(0)[1:])
    for _, hd, hparts in pending:
        finish(hd, hparts)

    o_ref[0] = x_ref[0] + g_ref[0] * acc[0]


def _mix_ab(q2, lat_kv, ctx_kv, gb, p, conv_w, lam_qk, subln_w, w_out, x, gate, lam_init, tq):
    b, s, _ = x.shape
    nq = s // tq
    nh = s // HALO
    per_batch = gate.shape[0] > 1
    has_lat = lat_kv is not None
    bmap = lambda bb, i: (bb, 0, 0)
    in_specs = [pl.BlockSpec((1, 2, tq, A_WIDTH), lambda bb, i: (bb, 0, i, 0))]
    args = [q2]
    for kv in ([lat_kv] if has_lat else []) + [ctx_kv]:
        kt, vx = kv
        in_specs += [pl.BlockSpec((1,) + kt.shape[1:], bmap), pl.BlockSpec((1,) + vx.shape[1:], bmap)]
        args += [kt, vx]
    tile = pl.BlockSpec((1, tq, A_WIDTH), lambda bb, i: (bb, i, 0))
    in_specs += [
        tile, tile,
        pl.BlockSpec((1, HALO, A_WIDTH), lambda bb, i: (bb, jnp.maximum(i * (tq // HALO) - 1, 0), 0)),
        pl.BlockSpec((1, HALO, A_WIDTH), lambda bb, i: (bb, jnp.minimum((i + 1) * (tq // HALO), nh - 1), 0)),
        _const_spec(conv_w.shape), _const_spec(lam_qk.shape), _const_spec((1, B_V_DIM)),
        _const_spec(w_out.shape),
        pl.BlockSpec((1, tq, D_MODEL), lambda bb, i: (bb, i, 0)),
        pl.BlockSpec((1, 1, D_MODEL), bmap if per_batch else (lambda bb, i: (0, 0, 0))),
    ]
    args += [gb, p, p, p, conv_w, lam_qk, subln_w.reshape(1, B_V_DIM), w_out, x, gate]
    return pl.pallas_call(
        functools.partial(_mix_ab_kernel, has_lat=has_lat, lam_init=lam_init),
        grid=(b, nq), in_specs=in_specs,
        out_specs=pl.BlockSpec((1, tq, D_MODEL), lambda bb, i: (bb, i, 0)),
        out_shape=jax.ShapeDtypeStruct(x.shape, F32),
        compiler_params=_cparams(2), name="mix_ab_lat" if has_lat else "mix_ab_ctx",
    )(*args)


def _ffn_kernel(*refs, final_norm):
    if final_norm:
        (x_ref, xp_ref, xn_ref, sh_ref, sc_ref, g_ref, wu_ref, cw_ref, cb_ref, wd_ref, fw_ref,
         o_ref, act_scr) = refs
    else:
        (x_ref, xp_ref, xn_ref, sh_ref, sc_ref, g_ref, wu_ref, cw_ref, cb_ref, wd_ref,
         o_ref, act_scr) = refs
    i = pl.program_id(1)
    last = pl.num_programs(1) - 1
    tm = x_ref.shape[1]
    rows = tm + 2 * HALO
    sh, sc = sh_ref[0], sc_ref[0]
    x = x_ref[0]
    hp = jnp.where(i > 0, _norm_mod(xp_ref[0], sh, sc), 0.0)
    hn = jnp.where(i < last, _norm_mod(xn_ref[0], sh, sc), 0.0)
    hext = jnp.concatenate([hp, _norm_mod(x, sh, sc), hn], axis=0).astype(BF16)

    for c in range(D_FF // FF_CHUNK):
        halves = []
        for part in range(2):
            col = part * D_FF + c * FF_CHUNK
            u = jnp.dot(hext, wu_ref[0, :, col:col + FF_CHUNK], preferred_element_type=F32)
            cw = cw_ref[:, col:col + FF_CHUNK]
            halves.append(pltpu.roll(u, 1, 0)[HALO:HALO + tm] * cw[0:1]
                          + u[HALO:HALO + tm] * cw[1:2]
                          + pltpu.roll(u, rows - 1, 0)[HALO:HALO + tm] * cw[2:3]
                          + cb_ref[:, col:col + FF_CHUNK])
        a, g = halves
        act_scr[:, c * FF_CHUNK:(c + 1) * FF_CHUNK] = (g * _sigmoid(g) * a).astype(BF16)

    y = jnp.dot(act_scr[...], wd_ref[0], preferred_element_type=F32)
    out = x + g_ref[0] * y
    if final_norm:
        ms = jnp.mean(out * out, axis=-1, keepdims=True)
        out = out * lax.rsqrt(ms + EPS) * fw_ref[...]
    o_ref[0] = out


def _ffn(x, shift, scale, gate, layer, w_up, conv_w, conv_b, w_down, final_w, tm):
    b, s, _ = x.shape
    nt = s // tm
    nh = s // HALO
    per_batch = shift.shape[0] > 1
    final_norm = final_w is not None
    mod_spec = pl.BlockSpec((1, 1, D_MODEL), (lambda bb, i: (bb, 0, 0)) if per_batch else (lambda bb, i: (0, 0, 0)))
    in_specs = [
        pl.BlockSpec((1, tm, D_MODEL), lambda bb, i: (bb, i, 0)),
        pl.BlockSpec((1, HALO, D_MODEL), lambda bb, i: (bb, jnp.maximum(i * (tm // HALO) - 1, 0), 0)),
        pl.BlockSpec((1, HALO, D_MODEL), lambda bb, i: (bb, jnp.minimum((i + 1) * (tm // HALO), nh - 1), 0)),
        mod_spec, mod_spec, mod_spec,
        _layer_spec(w_up.shape, layer), _const_spec(conv_w.shape), _const_spec((1, 2 * D_FF)),
        _layer_spec(w_down.shape, layer),
    ]
    args = [x, x, x, shift, scale, gate, w_up, conv_w, conv_b.reshape(1, 2 * D_FF), w_down]
    if final_norm:
        in_specs.append(_const_spec((1, D_MODEL)))
        args.append(final_w.reshape(1, D_MODEL))
    return pl.pallas_call(
        functools.partial(_ffn_kernel, final_norm=final_norm),
        grid=(b, nt), in_specs=in_specs,
        out_specs=pl.BlockSpec((1, tm, D_MODEL), lambda bb, i: (bb, i, 0)),
        out_shape=jax.ShapeDtypeStruct(x.shape, F32),
        scratch_shapes=[pltpu.VMEM((tm, D_FF), BF16)],
        compiler_params=_cparams(2), name="ffn_final" if final_norm else "ffn",
    )(*args)


def _dup_halves(a):
    sw = pltpu.roll(a, HEAD_DIM, 1)
    first = lax.broadcasted_iota(jnp.int32, a.shape, 1) < HEAD_DIM
    return jnp.where(first, a, sw), jnp.where(first, sw, a)


def _proj_cd_kernel(*refs, rope, kv_only):
    if kv_only:
        x_ref, sh_ref, sc_ref, w_ref, kt_ref, v_ref = refs
    else:
        x_ref, sh_ref, sc_ref, w_ref, c_ref, s1_ref, s2_ref, q_ref, kt_ref, v_ref, pl_ref = refs
    h = _norm_mod(x_ref[0], sh_ref[0], sc_ref[0]).astype(BF16)
    kv = jnp.dot(h, w_ref[:, C_Q_WIDTH:C_Q_WIDTH + 2 * C_KV_WIDTH], preferred_element_type=F32)
    k = kv[:, :C_KV_WIDTH]
    v = kv[:, C_KV_WIDTH:]
    if rope:
        c, s1, s2 = c_ref[...], s1_ref[...], s2_ref[...]
        k = _rope(k, c, s1, s2)
    k0, k1 = _dup_halves(k)
    kt_ref[0, 0] = k0.T.astype(BF16)
    kt_ref[0, 1] = k1.T.astype(BF16)
    v0, v1 = _dup_halves(v)
    ones = jnp.ones(v.shape, BF16)
    v_ref[0] = jnp.concatenate([v0.astype(BF16), ones, v1.astype(BF16), ones], axis=1)
    if not kv_only:
        q = jnp.dot(h, w_ref[:, :C_Q_WIDTH], preferred_element_type=F32)
        if rope:
            q = _rope(q, c, s1, s2)
        q_ref[0] = (q * QSCALE).astype(BF16)
        pl_ref[0] = jnp.dot(h, w_ref[:, C_Q_WIDTH + 2 * C_KV_WIDTH:], preferred_element_type=F32)


def _proj_cd(x, shift, scale, w, tables, tm):
    b, s, _ = x.shape
    nt = s // tm
    kv_only = tables is None
    per_batch = shift.shape[0] > 1
    mod_spec = pl.BlockSpec((1, 1, D_MODEL), (lambda i, bb: (bb, 0, 0)) if per_batch else (lambda i, bb: (0, 0, 0)))
    in_specs = [pl.BlockSpec((1, tm, D_MODEL), lambda i, bb: (bb, i, 0)), mod_spec, mod_spec,
                _const_spec(w.shape)]
    args = [x, shift, scale, w]
    kv_shapes = (
        jax.ShapeDtypeStruct((b, C_KV_HEADS, LANES, s), BF16),
        jax.ShapeDtypeStruct((b, s, 2 * C_KV_HEADS * LANES), BF16),
    )
    kv_specs = (
        pl.BlockSpec((1, C_KV_HEADS, LANES, tm), lambda i, bb: (bb, 0, 0, i)),
        pl.BlockSpec((1, tm, 2 * C_KV_HEADS * LANES), lambda i, bb: (bb, i, 0)),
    )
    if kv_only:
        out_shape, out_specs = kv_shapes, kv_specs
    else:
        in_specs += [pl.BlockSpec((tm, LANES), lambda i, bb: (i, 0))] * 3
        args += list(tables)
        out_shape = (jax.ShapeDtypeStruct((b, s, C_Q_WIDTH), BF16),) + kv_shapes + (
            jax.ShapeDtypeStruct((b, s, D_WIDTH), F32),)
        out_specs = (pl.BlockSpec((1, tm, C_Q_WIDTH), lambda i, bb: (bb, i, 0)),) + kv_specs + (
            pl.BlockSpec((1, tm, D_WIDTH), lambda i, bb: (bb, i, 0)),)
    return pl.pallas_call(
        functools.partial(_proj_cd_kernel, rope=not kv_only, kv_only=kv_only),
        grid=(nt, b), in_specs=in_specs, out_specs=out_specs, out_shape=out_shape,
        compiler_params=_cparams(2), name="proj_cd_kv" if kv_only else "proj_cd",
    )(*args)


def _mix_cd_kernel(q_ref, ktp_ref, ktm_ref, ktn_ref, ktc_ref, vp_ref, vm_ref, vn_ref, vc_ref,
                   pl_ref, plp_ref, pln_ref, sink_ref, pw_ref, ps_ref, wo_ref, x_ref, g_ref, o_ref):
    i = pl.program_id(1)
    last = pl.num_programs(1) - 1
    tq = x_ref.shape[1]
    group = C_HEADS // C_KV_HEADS

    t = lax.broadcasted_iota(jnp.int32, (tq, tq + 2 * WINDOW), 0)
    cidx = lax.broadcasted_iota(jnp.int32, (tq, tq + 2 * WINDOW), 1)
    rel = cidx - WINDOW - t
    c_lo = jnp.where(i > 0, 0, WINDOW)
    c_hi = jnp.where(i < last, tq + 2 * WINDOW, tq + WINDOW)
    bias = jnp.where(jnp.abs(rel) <= WINDOW, 0.0, NEG_INF)
    bias = jnp.where(cidx >= c_lo, bias, NEG_INF)
    bias = jnp.where(cidx < c_hi, bias, NEG_INF)
    bias = jnp.concatenate([bias] * group, axis=0)

    q = q_ref[0]
    first = (lax.broadcasted_iota(jnp.int32, (tq, LANES), 1) < HEAD_DIM)
    zero = jnp.zeros((tq, LANES), BF16)
    kvs = range(C_KV_HEADS)
    qs = []
    for j in kvs:
        rows = []
        for cc in range(group // 2):
            qc = q[:, (j * (group // 2) + cc) * LANES:(j * (group // 2) + cc + 1) * LANES]
            rows += [jnp.where(first, qc, zero), jnp.where(first, zero, qc)]
        qs.append(jnp.concatenate(rows, axis=0))
    s_c = [jnp.dot(qs[j], ktc_ref[0, j], preferred_element_type=F32) for j in kvs]
    kt_band = [jnp.concatenate([ktp_ref[0, j], ktm_ref[0, j], ktn_ref[0, j]], axis=1) for j in kvs]
    s_w = [jnp.dot(qs[j], kt_band[j], preferred_element_type=F32) + bias for j in kvs]

    pm = pl_ref[0]
    pe = jnp.concatenate([jnp.where(i > 0, plp_ref[0], 0.0), pm, jnp.where(i < last, pln_ref[0], 0.0)], axis=0)
    rows_e = tq + 2 * HALO
    n = tq * pl.num_programs(1)
    pos = i * tq + lax.broadcasted_iota(jnp.int32, (tq, POOL_GROUP), 0)
    yd = []
    for gi, w in enumerate(POOL_SIZES):
        glo, ghi = gi * POOL_GROUP, (gi + 1) * POOL_GROUP
        pg = pe[:, glo:ghi]
        acc = pltpu.roll(pg, 1, 0) + pg
        half = 1
        while 2 * half < w:
            acc = pltpu.roll(acc, half, 0) + pltpu.roll(acc, rows_e - half, 0)
            half *= 2
        cnt = (jnp.minimum(pos + w // 2, n) - jnp.maximum(pos - w // 2, 0)).astype(F32)
        z = (acc[HALO:HALO + tq] / cnt - pm[:, glo:ghi]).astype(BF16)
        yd.append(jnp.dot(z, pw_ref[gi], preferred_element_type=F32) * ps_ref[:, glo:ghi])
    y = jnp.dot(jnp.concatenate(yd, axis=1).astype(BF16), wo_ref[C_Q_WIDTH:, :], preferred_element_type=F32)

    sink = [jnp.concatenate(
        [jnp.broadcast_to(sink_ref[:, j * group + r:j * group + r + 1] * LOG2E, (tq, 1)) for r in range(group)],
        axis=0) for j in kvs]
    m = [jnp.maximum(jnp.maximum(jnp.max(s_c[j], axis=-1, keepdims=True), jnp.max(s_w[j], axis=-1, keepdims=True)),
                     sink[j]) for j in kvs]
    p_c = [jnp.exp2(s_c[j] - m[j]).astype(BF16) for j in kvs]
    p_w = [jnp.exp2(s_w[j] - m[j]).astype(BF16) for j in kvs]
    vcols = [slice(j * 2 * LANES, (j + 1) * 2 * LANES) for j in kvs]
    o = [jnp.dot(p_c[j], vc_ref[0, :, vcols[j]], preferred_element_type=F32) for j in kvs]
    v_band = [jnp.concatenate([vp_ref[0, :, vcols[j]], vm_ref[0, :, vcols[j]], vn_ref[0, :, vcols[j]]], axis=0)
              for j in kvs]
    for j in kvs:
        o[j] = o[j] + jnp.dot(p_w[j], v_band[j], preferred_element_type=F32)
    ya = []
    for j in kvs:
        den = o[j][:, LANES:LANES + 1] + jnp.exp2(sink[j] - m[j])
        oh = o[j][:, :LANES] / den
        ya += [jnp.where(first, oh[(2 * cc) * tq:(2 * cc + 1) * tq], oh[(2 * cc + 1) * tq:(2 * cc + 2) * tq])
               for cc in range(group // 2)]
    y = y + jnp.dot(jnp.concatenate(ya, axis=1).astype(BF16), wo_ref[:C_Q_WIDTH, :], preferred_element_type=F32)

    o_ref[0] = x_ref[0] + g_ref[0] * y


def _mix_cd(q, kt, vx, ktc, vxc, plv, sink, pool_w, pool_scale, w_out, x, gate, tq):
    b, s, _ = x.shape
    nq = s // tq
    nh = s // HALO
    nw = s // WINDOW
    r = tq // WINDOW
    lc = ktc.shape[-1]
    vw = vx.shape[-1]
    prev_w = lambda bb, i: jnp.maximum(i * r - 1, 0)
    next_w = lambda bb, i: jnp.minimum((i + 1) * r, nw - 1)
    in_specs = [
        pl.BlockSpec((1, tq, C_Q_WIDTH), lambda bb, i: (bb, i, 0)),
        pl.BlockSpec((1, C_KV_HEADS, LANES, WINDOW), lambda bb, i: (bb, 0, 0, prev_w(bb, i))),
        pl.BlockSpec((1, C_KV_HEADS, LANES, tq), lambda bb, i: (bb, 0, 0, i)),
        pl.BlockSpec((1, C_KV_HEADS, LANES, WINDOW), lambda bb, i: (bb, 0, 0, next_w(bb, i))),
        pl.BlockSpec((1, C_KV_HEADS, LANES, lc), lambda bb, i: (bb, 0, 0, 0)),
        pl.BlockSpec((1, WINDOW, vw), lambda bb, i: (bb, prev_w(bb, i), 0)),
        pl.BlockSpec((1, tq, vw), lambda bb, i: (bb, i, 0)),
        pl.BlockSpec((1, WINDOW, vw), lambda bb, i: (bb, next_w(bb, i), 0)),
        pl.BlockSpec((1, lc, vw), lambda bb, i: (bb, 0, 0)),
        pl.BlockSpec((1, tq, D_WIDTH), lambda bb, i: (bb, i, 0)),
        pl.BlockSpec((1, HALO, D_WIDTH), lambda bb, i: (bb, jnp.maximum(i * (tq // HALO) - 1, 0), 0)),
        pl.BlockSpec((1, HALO, D_WIDTH), lambda bb, i: (bb, jnp.minimum((i + 1) * (tq // HALO), nh - 1), 0)),
        _const_spec((1, C_HEADS)), _const_spec(pool_w.shape), _const_spec((1, D_WIDTH)), _const_spec(w_out.shape),
        pl.BlockSpec((1, tq, D_MODEL), lambda bb, i: (bb, i, 0)),
        pl.BlockSpec((1, 1, D_MODEL), lambda bb, i: (bb, 0, 0)),
    ]
    args = [q, kt, kt, kt, ktc, vx, vx, vx, vxc, plv, plv, plv, sink.reshape(1, C_HEADS), pool_w,
            pool_scale.reshape(1, D_WIDTH), w_out, x, gate]
    return pl.pallas_call(
        _mix_cd_kernel,
        grid=(b, nq), in_specs=in_specs,
        out_specs=pl.BlockSpec((1, tq, D_MODEL), lambda bb, i: (bb, i, 0)),
        out_shape=jax.ShapeDtypeStruct(x.shape, F32),
        compiler_params=_cparams(2), name="mix_cd",
    )(*args)


def kernel(x, c, ctx, c_ctx, w_mod, b_mod, w_in_ab, conv_a, lam_qk, subln_b, w_out_ab, w_in_cd, sink_c, pool_w,
           pool_scale, w_out_cd, w_up, conv_ffn_w, conv_ffn_b, w_down, final_norm_w):
    b, s, d = x.shape
    lc = ctx.shape[1]
    assert d == D_MODEL and DEPTH == 2 and all(s % t == 0 for t in (TM_PROJ, TM_FFN, TQ_AB, TQ_ATT))

    rows = ((b + 1 + HALO - 1) // HALO) * HALO
    cvec = jnp.concatenate([c, c_ctx[None, :], jnp.zeros((rows - b - 1, d), F32)], axis=0)
    mods = _mods(cvec, w_mod, b_mod)

    def split(l):
        ml = [mods[l, :b, j * d:(j + 1) * d].reshape(b, 1, d) for j in range(6)]
        mc = [mods[l, b:b + 1, j * d:(j + 1) * d].reshape(1, 1, d) for j in range(6)]
        return ml, mc

    tables = _rope_tables(s)
    bf = lambda a: a.astype(BF16)

    ml, mc = split(0)
    w_in = bf(w_in_ab[0])
    w_out = bf(w_out_ab[0])
    gb, p, q2, kt, vx = _proj_ab(x, ml[0], ml[1], w_in, tables, TM_PROJ)
    gbc, pc, q2c, ktc, vxc = _proj_ab(ctx, mc[0], mc[1], w_in, None, lc)
    lam0 = _lambda_init(0)
    xl = _mix_ab(q2, (kt, vx), (ktc, vxc), gb, p, conv_a[0], lam_qk[0], subln_b[0], w_out, x, ml[2], lam0, TQ_AB)
    xc = _mix_ab(q2c, None, (ktc, vxc), gbc, pc, conv_a[0], lam_qk[0], subln_b[0], w_out, ctx, mc[2], lam0, lc)
    w_up_bf, w_down_bf = bf(w_up), bf(w_down)
    ffn_w = (0, w_up_bf, conv_ffn_w[0], conv_ffn_b[0], w_down_bf)
    xl = _ffn(xl, ml[3], ml[4], ml[5], *ffn_w, None, TM_FFN)
    xc = _ffn(xc, mc[3], mc[4], mc[5], *ffn_w, None, lc)

    ml, mc = split(1)
    w_in = bf(w_in_cd[0])
    q, kt, vx, plv = _proj_cd(xl, ml[0], ml[1], w_in, tables, TM_PROJ)
    ktc, vxc = _proj_cd(xc, mc[0], mc[1], w_in, None, lc)
    xl = _mix_cd(q, kt, vx, ktc, vxc, plv, sink_c[0], bf(pool_w[0]), pool_scale[0], bf(w_out_cd[0]), xl, ml[2],
                 TQ_ATT)
    return _ffn(xl, ml[3], ml[4], ml[5], 1, w_up_bf, conv_ffn_w[1], conv_ffn_b[1], w_down_bf, final_norm_w, TM_FFN)
```

```python
import functools
import math

import jax
import jax.numpy as jnp
from jax import lax
from jax.experimental import pallas as pl
from jax.experimental.pallas import tpu as pltpu

F32 = jnp.float32
BF16 = jnp.bfloat16

D_MODEL = 1024
DEPTH = 2
GRID_W = 64
HEAD_DIM = 64
ROPE_THETA = 10000.0
EPS = 1e-6
NEG_INF = -1e30
WINDOW = 128
A_WIDTH = D_MODEL // 2
B_HEADS = D_MODEL // 256
B_V_DIM = 2 * HEAD_DIM
C_HEADS = D_MODEL // 128
C_KV_HEADS = C_HEADS // 4
C_Q_WIDTH = C_HEADS * HEAD_DIM
C_KV_WIDTH = C_KV_HEADS * HEAD_DIM
POOL_SIZES = (2, 4, 8, 16)
D_WIDTH = D_MODEL // 2
POOL_GROUP = D_WIDTH // len(POOL_SIZES)
D_FF = ((8 * D_MODEL // 3 + 127) // 128) * 128

LANES = 128
HALO = 8
LOG2E = 1.4426950408889634
QSCALE = (HEAD_DIM ** -0.5) * LOG2E
VMEM_LIMIT = 56 * 1024 * 1024

TM_PROJ = 1024
TQ_AB = 512
TQ_ATT = 256
TM_FFN = 512
CTX_PER_STEP = 2
FF_CHUNK = 256
KEY_CHUNK = 1024
FINISH_DELAY = 1


def _lambda_init(layer):
    return 0.8 - 0.6 * math.exp(-0.3 * layer)


def _cparams(n_axes):
    return pltpu.CompilerParams(dimension_semantics=("arbitrary",) * n_axes,
                                vmem_limit_bytes=VMEM_LIMIT)


def _const_spec(shape):
    nd = len(shape)
    return pl.BlockSpec(shape, lambda *_: (0,) * nd, pipeline_mode=pl.Buffered(1))


def _layer_spec(shape, layer):
    nd = len(shape)
    return pl.BlockSpec((1,) + tuple(shape[1:]), lambda *_: (layer,) + (0,) * (nd - 1),
                        pipeline_mode=pl.Buffered(1))


def _norm_mod(x, shift, scale):
    ms = jnp.mean(x * x, axis=-1, keepdims=True)
    return x * lax.rsqrt(ms + EPS) * (1.0 + scale) + shift


def _sigmoid(x):
    return 1.0 / (1.0 + jnp.exp(-x))


def _rope(x, c, s1, s2):
    outs = []
    for j in range(x.shape[1] // LANES):
        xc = x[:, j * LANES:(j + 1) * LANES]
        outs.append(xc * c + pltpu.roll(xc, LANES - 16, 1) * s1 + pltpu.roll(xc, 16, 1) * s2)
    return outs[0] if len(outs) == 1 else jnp.concatenate(outs, axis=1)


def _rope_tables(n):
    t = jnp.arange(n, dtype=jnp.int32)
    row = (t // GRID_W).astype(F32)
    col = (t % GRID_W).astype(F32)
    axis_dim = HEAD_DIM // 2
    inv = 1.0 / (ROPE_THETA ** (jnp.arange(0, axis_dim, 2, dtype=F32) / axis_dim))
    ar = row[:, None] * inv
    ac = col[:, None] * inv
    cr, sr, cc, sc = jnp.cos(ar), jnp.sin(ar), jnp.cos(ac), jnp.sin(ac)
    z = jnp.zeros_like(cr)
    c64 = jnp.concatenate([cr, cr, cc, cc], axis=-1)
    s1 = jnp.concatenate([-sr, z, -sc, z], axis=-1)
    s2 = jnp.concatenate([z, sr, z, sc], axis=-1)
    reps = LANES // HEAD_DIM
    return tuple(jnp.tile(a, (1, reps)) for a in (c64, s1, s2))


def _mods_kernel(c_ref, w_ref, b_ref, o_ref):
    cv = c_ref[...]
    act = (cv * _sigmoid(cv)).astype(BF16)
    o_ref[0] = jnp.dot(act, w_ref[0].astype(BF16), preferred_element_type=F32) + b_ref[0]


def _mods(cvec, w_mod, b_mod):
    rows = cvec.shape[0]
    nblk = 6
    return pl.pallas_call(
        _mods_kernel,
        grid=(DEPTH, nblk),
        in_specs=[
            pl.BlockSpec((rows, D_MODEL), lambda l, j: (0, 0)),
            pl.BlockSpec((1, D_MODEL, D_MODEL), lambda l, j: (l, 0, j)),
            pl.BlockSpec((1, 1, D_MODEL), lambda l, j: (l, 0, j)),
        ],
        out_specs=pl.BlockSpec((1, rows, D_MODEL), lambda l, j: (l, 0, j)),
        out_shape=jax.ShapeDtypeStruct((DEPTH, rows, 6 * D_MODEL), F32),
        compiler_params=_cparams(2),
        name="mods",
    )(cvec, w_mod, b_mod.reshape(DEPTH, 1, 6 * D_MODEL))


def _proj_ab_kernel(*refs, rope):
    if rope:
        x_ref, sh_ref, sc_ref, w_ref, c_ref, s1_ref, s2_ref, gb_ref, p_ref, q_ref, kt_ref, v_ref = refs
    else:
        x_ref, sh_ref, sc_ref, w_ref, gb_ref, p_ref, q_ref, kt_ref, v_ref = refs
    h = _norm_mod(x_ref[0], sh_ref[0], sc_ref[0]).astype(BF16)

    def proj(j):
        return jnp.dot(h, w_ref[:, j * A_WIDTH:(j + 1) * A_WIDTH], preferred_element_type=F32)

    gb_ref[0] = proj(0)
    p_ref[0] = proj(1) * proj(2)
    q = proj(3)
    k = proj(4)
    if rope:
        c, s1, s2 = c_ref[...], s1_ref[...], s2_ref[...]
        q = _rope(q, c, s1, s2)
        k = _rope(k, c, s1, s2)
    q = q * QSCALE
    first_map = (lax.broadcasted_iota(jnp.int32, q.shape, 1) & HEAD_DIM) == 0
    q_ref[0, 0] = jnp.where(first_map, q, 0.0).astype(BF16)
    q_ref[0, 1] = jnp.where(first_map, 0.0, q).astype(BF16)
    kt_ref[0] = k.T.astype(BF16)
    v = proj(5).astype(BF16)
    ones = jnp.ones((v.shape[0], LANES), BF16)
    pieces = []
    for hd in range(B_HEADS):
        pieces += [v[:, hd * B_V_DIM:(hd + 1) * B_V_DIM], ones]
    v_ref[0] = jnp.concatenate(pieces, axis=1)


def _proj_ab(x, shift, scale, w, tables, tm):
    b, s, _ = x.shape
    nt = s // tm
    rope = tables is not None
    per_batch = shift.shape[0] > 1
    mod_spec = pl.BlockSpec((1, 1, D_MODEL), (lambda i, bb: (bb, 0, 0)) if per_batch else (lambda i, bb: (0, 0, 0)))
    in_specs = [pl.BlockSpec((1, tm, D_MODEL), lambda i, bb: (bb, i, 0)), mod_spec, mod_spec,
                _const_spec(w.shape)]
    args = [x, shift, scale, w]
    if rope:
        in_specs += [pl.BlockSpec((tm, LANES), lambda i, bb: (i, 0))] * 3
        args += list(tables)
    out_shape = (
        jax.ShapeDtypeStruct((b, s, A_WIDTH), F32),
        jax.ShapeDtypeStruct((b, s, A_WIDTH), F32),
        jax.ShapeDtypeStruct((b, 2, s, A_WIDTH), BF16),
        jax.ShapeDtypeStruct((b, A_WIDTH, s), BF16),
        jax.ShapeDtypeStruct((b, s, 2 * A_WIDTH), BF16),
    )
    out_specs = (
        pl.BlockSpec((1, tm, A_WIDTH), lambda i, bb: (bb, i, 0)),
        pl.BlockSpec((1, tm, A_WIDTH), lambda i, bb: (bb, i, 0)),
        pl.BlockSpec((1, 2, tm, A_WIDTH), lambda i, bb: (bb, 0, i, 0)),
        pl.BlockSpec((1, A_WIDTH, tm), lambda i, bb: (bb, 0, i)),
        pl.BlockSpec((1, tm, 2 * A_WIDTH), lambda i, bb: (bb, i, 0)),
    )
    return pl.pallas_call(
        functools.partial(_proj_ab_kernel, rope=rope),
        grid=(nt, b), in_specs=in_specs, out_specs=out_specs, out_shape=out_shape,
        compiler_params=_cparams(2), name="proj_ab_rope" if rope else "proj_ab",
    )(*args)


def _halo_conv3(pm, prev_row, next_row, w):
    rows = pm.shape[0]
    ridx = lax.broadcasted_iota(jnp.int32, pm.shape, 0)
    dn = jnp.where(ridx == 0, prev_row, pltpu.roll(pm, 1, 0))
    up = jnp.where(ridx == rows - 1, next_row, pltpu.roll(pm, rows - 1, 0))
    return dn * w[0:1] + pm * w[1:2] + up * w[2:3]


def _mix_ab_kernel(*refs, has_lat, lam_init):
    if has_lat:
        (q_ref, ktl_ref, vl_ref, ktc_ref, vc_ref, gb_ref, p_ref, pp_ref, pn_ref, cw_ref, lq_ref, sl_ref,
         wo_ref, x_ref, g_ref, o_ref) = refs
        keys = ((ktl_ref, vl_ref), (ktc_ref, vc_ref))
    else:
        (q_ref, ktc_ref, vc_ref, gb_ref, p_ref, pp_ref, pn_ref, cw_ref, lq_ref, sl_ref,
         wo_ref, x_ref, g_ref, o_ref) = refs
        keys = ((ktc_ref, vc_ref),)
    i = pl.program_id(1)
    last = pl.num_programs(1) - 1
    tq = x_ref.shape[1]

    lq = lq_ref[...]
    lam = (jnp.exp(jnp.sum(lq[0:1] * lq[1:2], axis=-1, keepdims=True))
           - jnp.exp(jnp.sum(lq[2:3] * lq[3:4], axis=-1, keepdims=True)) + lam_init)

    prev_row = jnp.where(i > 0, pp_ref[0, HALO - 1:HALO, :], 0.0)
    next_row = jnp.where(i < last, pn_ref[0, 0:1, :], 0.0)
    ya = gb_ref[0] * _halo_conv3(p_ref[0], prev_row, next_row, cw_ref[...])
    y = jnp.dot(ya.astype(BF16), wo_ref[0:A_WIDTH, :], preferred_element_type=F32)

    chunks = []
    for hd in range(B_HEADS):
        for kt, vr in keys:
            nk = kt.shape[2]
            for k0 in range(0, nk, KEY_CHUNK):
                chunks.append((hd, kt, vr, k0, min(k0 + KEY_CHUNK, nk)))

    def scores(ci):
        hd, kt, _, k0, k1 = chunks[ci]
        lo, hi = hd * B_V_DIM, (hd + 1) * B_V_DIM
        qs = q_ref[0, :, :, lo:hi].reshape(2 * tq, B_V_DIM)
        return jnp.dot(qs, kt[0, lo:hi, k0:k1], preferred_element_type=F32)

    def finish_head(hd, parts):
        m = functools.reduce(jnp.maximum, [mc for mc, _ in parts])
        o = None
        for mc, oc in parts:
            t = jnp.exp2(mc - m) * oc
            o = t if o is None else o + t
        o1 = o[:tq, :B_V_DIM] / o[:tq, B_V_DIM:B_V_DIM + 1]
        o2 = o[tq:, :B_V_DIM] / o[tq:, B_V_DIM:B_V_DIM + 1]
        oh = o1 - lam * o2
        ms = jnp.mean(oh * oh, axis=-1, keepdims=True)
        return oh * lax.rsqrt(ms + EPS) * sl_ref[...] * (1.0 - lam_init)

    yb = {}
    acc = [y]
    pending = []

    def finish(hd, parts):
        yb[hd] = finish_head(hd, parts)
        if hd % 2 == 1:
            w_lo = A_WIDTH + (hd - 1) * B_V_DIM
            pair = jnp.concatenate([yb[hd - 1], yb[hd]], axis=1).astype(BF16)
            acc[0] = acc[0] + jnp.dot(pair, wo_ref[w_lo:w_lo + 2 * B_V_DIM, :], preferred_element_type=F32)

    parts = []
    sc_next = scores(0)
    for ci, (hd, _, vr, k0, k1) in enumerate(chunks):
        sc = sc_next
        if ci + 1 < len(chunks):
            sc_next = scores(ci + 1)
        mc = jnp.max(sc, axis=-1, keepdims=True)
        pc = jnp.exp2(sc - mc).astype(BF16)
        vh = vr[0, k0:k1, 2 * hd * B_V_DIM:2 * (hd + 1) * B_V_DIM]
        parts.append((mc, jnp.concatenate(
            [jnp.dot(pc[mp * tq:(mp + 1) * tq], vh, preferred_element_type=F32) for mp in range(2)], axis=0)))
        if ci + 1 == len(chunks) or chunks[ci + 1][0] != hd:
            pending.append((ci + FINISH_DELAY, hd, parts))
            parts = []
        while pending and pending[0][0] <= ci:
            finish(*pending.pop(0)[1:])
    for _, hd, hparts in pending:
        finish(hd, hparts)

    o_ref[0] = x_ref[0] + g_ref[0] * acc[0]


def _mix_ab(q2, lat_kv, ctx_kv, gb, p, conv_w, lam_qk, subln_w, w_out, x, gate, lam_init, tq):
    b, s, _ = x.shape
    nq = s // tq
    nh = s // HALO
    per_batch = gate.shape[0] > 1
    has_lat = lat_kv is not None
    bmap = lambda bb, i: (bb, 0, 0)
    in_specs = [pl.BlockSpec((1, 2, tq, A_WIDTH), lambda bb, i: (bb, 0, i, 0))]
    args = [q2]
    for kv in ([lat_kv] if has_lat else []) + [ctx_kv]:
        kt, vx = kv
        in_specs += [pl.BlockSpec((1,) + kt.shape[1:], bmap), pl.BlockSpec((1,) + vx.shape[1:], bmap)]
        args += [kt, vx]
    tile = pl.BlockSpec((1, tq, A_WIDTH), lambda bb, i: (bb, i, 0))
    in_specs += [
        tile, tile,
        pl.BlockSpec((1, HALO, A_WIDTH), lambda bb, i: (bb, jnp.maximum(i * (tq // HALO) - 1, 0), 0)),
        pl.BlockSpec((1, HALO, A_WIDTH), lambda bb, i: (bb, jnp.minimum((i + 1) * (tq // HALO), nh - 1), 0)),
        _const_spec(conv_w.shape), _const_spec(lam_qk.shape), _const_spec((1, B_V_DIM)),
        _const_spec(w_out.shape),
        pl.BlockSpec((1, tq, D_MODEL), lambda bb, i: (bb, i, 0)),
        pl.BlockSpec((1, 1, D_MODEL), bmap if per_batch else (lambda bb, i: (0, 0, 0))),
    ]
    args += [gb, p, p, p, conv_w, lam_qk, subln_w.reshape(1, B_V_DIM), w_out, x, gate]
    return pl.pallas_call(
        functools.partial(_mix_ab_kernel, has_lat=has_lat, lam_init=lam_init),
        grid=(b, nq), in_specs=in_specs,
        out_specs=pl.BlockSpec((1, tq, D_MODEL), lambda bb, i: (bb, i, 0)),
        out_shape=jax.ShapeDtypeStruct(x.shape, F32),
        compiler_params=_cparams(2), name="mix_ab_lat" if has_lat else "mix_ab_ctx",
    )(*args)


def _ffn_kernel(*refs, final_norm):
    if final_norm:
        (x_ref, xp_ref, xn_ref, sh_ref, sc_ref, g_ref, wu_ref, cw_ref, cb_ref, wd_ref, fw_ref,
         o_ref, act_scr) = refs
    else:
        (x_ref, xp_ref, xn_ref, sh_ref, sc_ref, g_ref, wu_ref, cw_ref, cb_ref, wd_ref,
         o_ref, act_scr) = refs
    i = pl.program_id(1)
    last = pl.num_programs(1) - 1
    nb, tm, _ = x_ref.shape
    rows = tm + 2 * HALO
    sh, sc = sh_ref[0], sc_ref[0]
    pieces = []
    for sq in range(nb):
        hp = jnp.where(i > 0, _norm_mod(xp_ref[sq], sh, sc), 0.0)
        hn = jnp.where(i < last, _norm_mod(xn_ref[sq], sh, sc), 0.0)
        pieces += [hp, _norm_mod(x_ref[sq], sh, sc), hn]
    hext = jnp.concatenate(pieces, axis=0).astype(BF16)

    def tile_rows(a):
        kept = [a[sq * rows + HALO:sq * rows + HALO + tm] for sq in range(nb)]
        return kept[0] if nb == 1 else jnp.concatenate(kept, axis=0)

    for c in range(D_FF // FF_CHUNK):
        halves = []
        for part in range(2):
            col = part * D_FF + c * FF_CHUNK
            u = jnp.dot(hext, wu_ref[0, :, col:col + FF_CHUNK], preferred_element_type=F32)
            cw = cw_ref[:, col:col + FF_CHUNK]
            halves.append(tile_rows(pltpu.roll(u, 1, 0)) * cw[0:1]
                          + tile_rows(u) * cw[1:2]
                          + tile_rows(pltpu.roll(u, nb * rows - 1, 0)) * cw[2:3]
                          + cb_ref[:, col:col + FF_CHUNK])
        a, g = halves
        act_scr[:, c * FF_CHUNK:(c + 1) * FF_CHUNK] = (g * _sigmoid(g) * a).astype(BF16)

    y = jnp.dot(act_scr[...], wd_ref[0], preferred_element_type=F32)
    for sq in range(nb):
        out = x_ref[sq] + g_ref[0] * y[sq * tm:(sq + 1) * tm]
        if final_norm:
            ms = jnp.mean(out * out, axis=-1, keepdims=True)
            out = out * lax.rsqrt(ms + EPS) * fw_ref[...]
        o_ref[sq] = out


def _ffn(x, shift, scale, gate, layer, w_up, conv_w, conv_b, w_down, final_w, tm, nb=1):
    b, s, _ = x.shape
    nt = s // tm
    nh = s // HALO
    per_batch = shift.shape[0] > 1
    assert b % nb == 0 and not (per_batch and nb > 1)
    final_norm = final_w is not None
    mod_spec = pl.BlockSpec((1, 1, D_MODEL), (lambda bb, i: (bb, 0, 0)) if per_batch else (lambda bb, i: (0, 0, 0)))
    in_specs = [
        pl.BlockSpec((nb, tm, D_MODEL), lambda bb, i: (bb, i, 0)),
        pl.BlockSpec((nb, HALO, D_MODEL), lambda bb, i: (bb, jnp.maximum(i * (tm // HALO) - 1, 0), 0)),
        pl.BlockSpec((nb, HALO, D_MODEL), lambda bb, i: (bb, jnp.minimum((i + 1) * (tm // HALO), nh - 1), 0)),
        mod_spec, mod_spec, mod_spec,
        _layer_spec(w_up.shape, layer), _const_spec(conv_w.shape), _const_spec((1, 2 * D_FF)),
        _layer_spec(w_down.shape, layer),
    ]
    args = [x, x, x, shift, scale, gate, w_up, conv_w, conv_b.reshape(1, 2 * D_FF), w_down]
    if final_norm:
        in_specs.append(_const_spec((1, D_MODEL)))
        args.append(final_w.reshape(1, D_MODEL))
    return pl.pallas_call(
        functools.partial(_ffn_kernel, final_norm=final_norm),
        grid=(b // nb, nt), in_specs=in_specs,
        out_specs=pl.BlockSpec((nb, tm, D_MODEL), lambda bb, i: (bb, i, 0)),
        out_shape=jax.ShapeDtypeStruct(x.shape, F32),
        scratch_shapes=[pltpu.VMEM((nb * tm, D_FF), BF16)],
        compiler_params=_cparams(2), name="ffn_final" if final_norm else "ffn",
    )(*args)


def _dup_halves(a):
    sw = pltpu.roll(a, HEAD_DIM, 1)
    first = lax.broadcasted_iota(jnp.int32, a.shape, 1) < HEAD_DIM
    return jnp.where(first, a, sw), jnp.where(first, sw, a)


def _proj_cd_kernel(*refs, rope, kv_only):
    if kv_only:
        x_ref, sh_ref, sc_ref, w_ref, kt_ref, v_ref = refs
    else:
        x_ref, sh_ref, sc_ref, w_ref, c_ref, s1_ref, s2_ref, q_ref, kt_ref, v_ref, pl_ref = refs
    h = _norm_mod(x_ref[0], sh_ref[0], sc_ref[0]).astype(BF16)
    kv = jnp.dot(h, w_ref[:, C_Q_WIDTH:C_Q_WIDTH + 2 * C_KV_WIDTH], preferred_element_type=F32)
    k = kv[:, :C_KV_WIDTH]
    v = kv[:, C_KV_WIDTH:]
    if rope:
        c, s1, s2 = c_ref[...], s1_ref[...], s2_ref[...]
        k = _rope(k, c, s1, s2)
    k0, k1 = _dup_halves(k)
    kt_ref[0, 0] = k0.T.astype(BF16)
    kt_ref[0, 1] = k1.T.astype(BF16)
    v0, v1 = _dup_halves(v)
    ones = jnp.ones(v.shape, BF16)
    v_ref[0] = jnp.concatenate([v0.astype(BF16), ones, v1.astype(BF16), ones], axis=1)
    if not kv_only:
        q = jnp.dot(h, w_ref[:, :C_Q_WIDTH], preferred_element_type=F32)
        if rope:
            q = _rope(q, c, s1, s2)
        q_ref[0] = (q * QSCALE).astype(BF16)
        pl_ref[0] = jnp.dot(h, w_ref[:, C_Q_WIDTH + 2 * C_KV_WIDTH:], preferred_element_type=F32)


def _proj_cd(x, shift, scale, w, tables, tm):
    b, s, _ = x.shape
    nt = s // tm
    kv_only = tables is None
    per_batch = shift.shape[0] > 1
    mod_spec = pl.BlockSpec((1, 1, D_MODEL), (lambda i, bb: (bb, 0, 0)) if per_batch else (lambda i, bb: (0, 0, 0)))
    in_specs = [pl.BlockSpec((1, tm, D_MODEL), lambda i, bb: (bb, i, 0)), mod_spec, mod_spec,
                _const_spec(w.shape)]
    args = [x, shift, scale, w]
    kv_shapes = (
        jax.ShapeDtypeStruct((b, C_KV_HEADS, LANES, s), BF16),
        jax.ShapeDtypeStruct((b, s, 2 * C_KV_HEADS * LANES), BF16),
    )
    kv_specs = (
        pl.BlockSpec((1, C_KV_HEADS, LANES, tm), lambda i, bb: (bb, 0, 0, i)),
        pl.BlockSpec((1, tm, 2 * C_KV_HEADS * LANES), lambda i, bb: (bb, i, 0)),
    )
    if kv_only:
        out_shape, out_specs = kv_shapes, kv_specs
    else:
        in_specs += [pl.BlockSpec((tm, LANES), lambda i, bb: (i, 0))] * 3
        args += list(tables)
        out_shape = (jax.ShapeDtypeStruct((b, s, C_Q_WIDTH), BF16),) + kv_shapes + (
            jax.ShapeDtypeStruct((b, s, D_WIDTH), F32),)
        out_specs = (pl.BlockSpec((1, tm, C_Q_WIDTH), lambda i, bb: (bb, i, 0)),) + kv_specs + (
            pl.BlockSpec((1, tm, D_WIDTH), lambda i, bb: (bb, i, 0)),)
    return pl.pallas_call(
        functools.partial(_proj_cd_kernel, rope=not kv_only, kv_only=kv_only),
        grid=(nt, b), in_specs=in_specs, out_specs=out_specs, out_shape=out_shape,
        compiler_params=_cparams(2), name="proj_cd_kv" if kv_only else "proj_cd",
    )(*args)


def _mix_cd_kernel(q_ref, ktp_ref, ktm_ref, ktn_ref, ktc_ref, vp_ref, vm_ref, vn_ref, vc_ref,
                   pl_ref, plp_ref, pln_ref, sink_ref, pw_ref, ps_ref, wo_ref, x_ref, g_ref, o_ref):
    i = pl.program_id(1)
    last = pl.num_programs(1) - 1
    tq = x_ref.shape[1]
    group = C_HEADS // C_KV_HEADS

    t = lax.broadcasted_iota(jnp.int32, (tq, tq + 2 * WINDOW), 0)
    cidx = lax.broadcasted_iota(jnp.int32, (tq, tq + 2 * WINDOW), 1)
    rel = cidx - WINDOW - t
    c_lo = jnp.where(i > 0, 0, WINDOW)
    c_hi = jnp.where(i < last, tq + 2 * WINDOW, tq + WINDOW)
    bias = jnp.where(jnp.abs(rel) <= WINDOW, 0.0, NEG_INF)
    bias = jnp.where(cidx >= c_lo, bias, NEG_INF)
    bias = jnp.where(cidx < c_hi, bias, NEG_INF)
    bias = jnp.concatenate([bias] * group, axis=0)

    q = q_ref[0]
    first = (lax.broadcasted_iota(jnp.int32, (tq, LANES), 1) < HEAD_DIM)
    zero = jnp.zeros((tq, LANES), BF16)
    kvs = range(C_KV_HEADS)
    qs = []
    for j in kvs:
        rows = []
        for cc in range(group // 2):
            qc = q[:, (j * (group // 2) + cc) * LANES:(j * (group // 2) + cc + 1) * LANES]
            rows += [jnp.where(first, qc, zero), jnp.where(first, zero, qc)]
        qs.append(jnp.concatenate(rows, axis=0))
    s_c = [jnp.dot(qs[j], ktc_ref[0, j], preferred_element_type=F32) for j in kvs]
    kt_band = [jnp.concatenate([ktp_ref[0, j], ktm_ref[0, j], ktn_ref[0, j]], axis=1) for j in kvs]
    s_w = [jnp.dot(qs[j], kt_band[j], preferred_element_type=F32) + bias for j in kvs]

    pm = pl_ref[0]
    pe = jnp.concatenate([jnp.where(i > 0, plp_ref[0], 0.0), pm, jnp.where(i < last, pln_ref[0], 0.0)], axis=0)
    rows_e = tq + 2 * HALO
    n = tq * pl.num_programs(1)
    pos = i * tq + lax.broadcasted_iota(jnp.int32, (tq, POOL_GROUP), 0)
    yd = []
    for gi, w in enumerate(POOL_SIZES):
        glo, ghi = gi * POOL_GROUP, (gi + 1) * POOL_GROUP
        pg = pe[:, glo:ghi]
        acc = pltpu.roll(pg, 1, 0) + pg
        half = 1
        while 2 * half < w:
            acc = pltpu.roll(acc, half, 0) + pltpu.roll(acc, rows_e - half, 0)
            half *= 2
        cnt = (jnp.minimum(pos + w // 2, n) - jnp.maximum(pos - w // 2, 0)).astype(F32)
        z = (acc[HALO:HALO + tq] / cnt - pm[:, glo:ghi]).astype(BF16)
        yd.append(jnp.dot(z, pw_ref[gi], preferred_element_type=F32) * ps_ref[:, glo:ghi])
    y = jnp.dot(jnp.concatenate(yd, axis=1).astype(BF16), wo_ref[C_Q_WIDTH:, :], preferred_element_type=F32)

    sink = [jnp.concatenate(
        [jnp.broadcast_to(sink_ref[:, j * group + r:j * group + r + 1] * LOG2E, (tq, 1)) for r in range(group)],
        axis=0) for j in kvs]
    m = [jnp.maximum(jnp.maximum(jnp.max(s_c[j], axis=-1, keepdims=True), jnp.max(s_w[j], axis=-1, keepdims=True)),
                     sink[j]) for j in kvs]
    p_c = [jnp.exp2(s_c[j] - m[j]).astype(BF16) for j in kvs]
    p_w = [jnp.exp2(s_w[j] - m[j]).astype(BF16) for j in kvs]
    vcols = [slice(j * 2 * LANES, (j + 1) * 2 * LANES) for j in kvs]
    o = [jnp.dot(p_c[j], vc_ref[0, :, vcols[j]], preferred_element_type=F32) for j in kvs]
    v_band = [jnp.concatenate([vp_ref[0, :, vcols[j]], vm_ref[0, :, vcols[j]], vn_ref[0, :, vcols[j]]], axis=0)
              for j in kvs]
    for j in kvs:
        o[j] = o[j] + jnp.dot(p_w[j], v_band[j], preferred_element_type=F32)
    ya = []
    for j in kvs:
        den = o[j][:, LANES:LANES + 1] + jnp.exp2(sink[j] - m[j])
        oh = o[j][:, :LANES] / den
        ya += [jnp.where(first, oh[(2 * cc) * tq:(2 * cc + 1) * tq], oh[(2 * cc + 1) * tq:(2 * cc + 2) * tq])
               for cc in range(group // 2)]
    y = y + jnp.dot(jnp.concatenate(ya, axis=1).astype(BF16), wo_ref[:C_Q_WIDTH, :], preferred_element_type=F32)

    o_ref[0] = x_ref[0] + g_ref[0] * y


def _mix_cd(q, kt, vx, ktc, vxc, plv, sink, pool_w, pool_scale, w_out, x, gate, tq):
    b, s, _ = x.shape
    nq = s // tq
    nh = s // HALO
    nw = s // WINDOW
    r = tq // WINDOW
    lc = ktc.shape[-1]
    vw = vx.shape[-1]
    prev_w = lambda bb, i: jnp.maximum(i * r - 1, 0)
    next_w = lambda bb, i: jnp.minimum((i + 1) * r, nw - 1)
    in_specs = [
        pl.BlockSpec((1, tq, C_Q_WIDTH), lambda bb, i: (bb, i, 0)),
        pl.BlockSpec((1, C_KV_HEADS, LANES, WINDOW), lambda bb, i: (bb, 0, 0, prev_w(bb, i))),
        pl.BlockSpec((1, C_KV_HEADS, LANES, tq), lambda bb, i: (bb, 0, 0, i)),
        pl.BlockSpec((1, C_KV_HEADS, LANES, WINDOW), lambda bb, i: (bb, 0, 0, next_w(bb, i))),
        pl.BlockSpec((1, C_KV_HEADS, LANES, lc), lambda bb, i: (bb, 0, 0, 0)),
        pl.BlockSpec((1, WINDOW, vw), lambda bb, i: (bb, prev_w(bb, i), 0)),
        pl.BlockSpec((1, tq, vw), lambda bb, i: (bb, i, 0)),
        pl.BlockSpec((1, WINDOW, vw), lambda bb, i: (bb, next_w(bb, i), 0)),
        pl.BlockSpec((1, lc, vw), lambda bb, i: (bb, 0, 0)),
        pl.BlockSpec((1, tq, D_WIDTH), lambda bb, i: (bb, i, 0)),
        pl.BlockSpec((1, HALO, D_WIDTH), lambda bb, i: (bb, jnp.maximum(i * (tq // HALO) - 1, 0), 0)),
        pl.BlockSpec((1, HALO, D_WIDTH), lambda bb, i: (bb, jnp.minimum((i + 1) * (tq // HALO), nh - 1), 0)),
        _const_spec((1, C_HEADS)), _const_spec(pool_w.shape), _const_spec((1, D_WIDTH)), _const_spec(w_out.shape),
        pl.BlockSpec((1, tq, D_MODEL), lambda bb, i: (bb, i, 0)),
        pl.BlockSpec((1, 1, D_MODEL), lambda bb, i: (bb, 0, 0)),
    ]
    args = [q, kt, kt, kt, ktc, vx, vx, vx, vxc, plv, plv, plv, sink.reshape(1, C_HEADS), pool_w,
            pool_scale.reshape(1, D_WIDTH), w_out, x, gate]
    return pl.pallas_call(
        _mix_cd_kernel,
        grid=(b, nq), in_specs=in_specs,
        out_specs=pl.BlockSpec((1, tq, D_MODEL), lambda bb, i: (bb, i, 0)),
        out_shape=jax.ShapeDtypeStruct(x.shape, F32),
        compiler_params=_cparams(2), name="mix_cd",
    )(*args)


def kernel(x, c, ctx, c_ctx, w_mod, b_mod, w_in_ab, conv_a, lam_qk, subln_b, w_out_ab, w_in_cd, sink_c, pool_w,
           pool_scale, w_out_cd, w_up, conv_ffn_w, conv_ffn_b, w_down, final_norm_w):
    b, s, d = x.shape
    lc = ctx.shape[1]
    assert d == D_MODEL and DEPTH == 2 and all(s % t == 0 for t in (TM_PROJ, TM_FFN, TQ_AB, TQ_ATT))

    rows = ((b + 1 + HALO - 1) // HALO) * HALO
    cvec = jnp.concatenate([c, c_ctx[None, :], jnp.zeros((rows - b - 1, d), F32)], axis=0)
    mods = _mods(cvec, w_mod, b_mod)

    def split(l):
        ml = [mods[l, :b, j * d:(j + 1) * d].reshape(b, 1, d) for j in range(6)]
        mc = [mods[l, b:b + 1, j * d:(j + 1) * d].reshape(1, 1, d) for j in range(6)]
        return ml, mc

    tables = _rope_tables(s)
    bf = lambda a: a.astype(BF16)

    ml, mc = split(0)
    w_in = bf(w_in_ab[0])
    w_out = bf(w_out_ab[0])
    gb, p, q2, kt, vx = _proj_ab(x, ml[0], ml[1], w_in, tables, TM_PROJ)
    gbc, pc, q2c, ktc, vxc = _proj_ab(ctx, mc[0], mc[1], w_in, None, lc)
    lam0 = _lambda_init(0)
    xl = _mix_ab(q2, (kt, vx), (ktc, vxc), gb, p, conv_a[0], lam_qk[0], subln_b[0], w_out, x, ml[2], lam0, TQ_AB)
    xc = _mix_ab(q2c, None, (ktc, vxc), gbc, pc, conv_a[0], lam_qk[0], subln_b[0], w_out, ctx, mc[2], lam0, lc)
    w_up_bf, w_down_bf = bf(w_up), bf(w_down)
    ffn_w = (0, w_up_bf, conv_ffn_w[0], conv_ffn_b[0], w_down_bf)
    xl = _ffn(xl, ml[3], ml[4], ml[5], *ffn_w, None, TM_FFN)
    xc = _ffn(xc, mc[3], mc[4], mc[5], *ffn_w, None, lc, nb=CTX_PER_STEP)

    ml, mc = split(1)
    w_in = bf(w_in_cd[0])
    q, kt, vx, plv = _proj_cd(xl, ml[0], ml[1], w_in, tables, TM_PROJ)
    ktc, vxc = _proj_cd(xc, mc[0], mc[1], w_in, None, lc)
    xl = _mix_cd(q, kt, vx, ktc, vxc, plv, sink_c[0], bf(pool_w[0]), pool_scale[0], bf(w_out_cd[0]), xl, ml[2],
                 TQ_ATT)
    return _ffn(xl, ml[3], ml[4], ml[5], 1, w_up_bf, conv_ffn_w[1], conv_ffn_b[1], w_down_bf, final_norm_w, TM_FFN)
```
